```python
import math
import jax, jax.numpy as jnp
from jax import lax
import numpy as np

D_MODEL = 4096
BATCH = 1
SEQ = 8192
DEPTH = 4
DEC_BATCH = 4
DEC_SEQ = 4096
PAST_LEN = 128

HEAD_DIM = 128
N_MIXERS = 4
GRID_W = 64
Q_BLOCK = 128
RMS_EPS = 1e-6
MASK_VALUE = -1e30
A_HEADS = D_MODEL // HEAD_DIM
A_KV_HEADS = A_HEADS // 4
ROPE_THETA = 10000.0
B_HEADS = D_MODEL // (2 * HEAD_DIM)
C_CONFIGS = ((128, 1), (512, 4), (2048, 16))
C_HEADS = D_MODEL // (2 * HEAD_DIM)
D_HEADS = D_MODEL // HEAD_DIM
NA_ROWS = 8
NA_COLS = 16
D_FF = ((8 * D_MODEL + 3 * 256 - 1) // (3 * 256)) * 256

kernel_name = 'hybrid_bidir_encoder'


def rmsnorm(x, gain):
    xf = x.astype(jnp.float32)
    y = xf * lax.rsqrt(jnp.mean(xf * xf, axis=-1, keepdims=True) + RMS_EPS)
    return (y * gain.astype(jnp.float32)).astype(x.dtype)


def alibi_slopes(n):
    return 2.0 ** (-8.0 * jnp.arange(1, n + 1, dtype=jnp.float32) / n)


def query_blocks(q):
    b, s = q.shape[:2]
    q = q.reshape((b, s // Q_BLOCK, Q_BLOCK) + q.shape[2:])
    return jnp.moveaxis(q, 1, 0)


def merge_blocks(o):
    o = jnp.moveaxis(o, 0, 1)
    return o.reshape((o.shape[0], o.shape[1] * o.shape[2]) + o.shape[3:])


def axial_rope(x):
    s = x.shape[1]
    half = HEAD_DIM // 2
    inv = ROPE_THETA ** (-jnp.arange(0, half, 2, dtype=jnp.float32) / half)
    t = jnp.arange(s)
    ang_r = (t // GRID_W).astype(jnp.float32)[:, None] * inv
    ang_c = (t % GRID_W).astype(jnp.float32)[:, None] * inv

    def rot(xh, ang):
        x1, x2 = xh[..., : half // 2], xh[..., half // 2:]
        c = jnp.cos(ang)[None, :, None, :]
        sn = jnp.sin(ang)[None, :, None, :]
        return jnp.concatenate([x1 * c - x2 * sn, x1 * sn + x2 * c], axis=-1)

    xf = x.astype(jnp.float32)
    out = jnp.concatenate([rot(xf[..., :half], ang_r), rot(xf[..., half:], ang_c)], axis=-1)
    return out.astype(x.dtype)


def mixer_gqa_axial(h, w_qkv, q_gain, k_gain, w_o):
    b, s, _ = h.shape
    qkv = h @ w_qkv
    q, k, v = jnp.split(qkv, [A_HEADS * HEAD_DIM, (A_HEADS + A_KV_HEADS) * HEAD_DIM], axis=-1)
    q = axial_rope(rmsnorm(q.reshape(b, s, A_HEADS, HEAD_DIM), q_gain))
    k = axial_rope(rmsnorm(k.reshape(b, s, A_KV_HEADS, HEAD_DIM), k_gain))
    v = v.reshape(b, s, A_KV_HEADS, HEAD_DIM)
    q = q.reshape(b, s, A_KV_HEADS, A_HEADS // A_KV_HEADS, HEAD_DIM)
    scale = HEAD_DIM ** -0.5

    def block(qb):
        sc = jnp.einsum('bqgrd,bkgd->bgrqk', qb, k, preferred_element_type=jnp.float32) * scale
        p = jax.nn.softmax(sc, axis=-1).astype(v.dtype)
        return jnp.einsum('bgrqk,bkgd->bqgrd', p, v)

    o = merge_blocks(lax.map(block, query_blocks(q)))
    return o.reshape(b, s, A_HEADS * HEAD_DIM) @ w_o


def mixer_differential(h, w_qkv, lq1, lk1, lq2, lk2, subln_gain, w_o, layer_idx):
    b, s, _ = h.shape
    lambda_init = 0.8 - 0.6 * math.exp(-0.3 * layer_idx)
    q, k, v = jnp.split(h @ w_qkv, 3, axis=-1)
    q = q.reshape(b, s, B_HEADS, 2, HEAD_DIM)
    k = k.reshape(b, s, B_HEADS, 2, HEAD_DIM)
    v = v.reshape(b, s, B_HEADS, 2 * HEAD_DIM)
    f = lambda a: a.astype(jnp.float32)
    lam = jnp.exp(jnp.sum(f(lq1) * f(lk1))) - jnp.exp(jnp.sum(f(lq2) * f(lk2))) + lambda_init
    slopes = alibi_slopes(B_HEADS)
    kpos = jnp.arange(s)
    scale = HEAD_DIM ** -0.5

    def block(args):
        qb, start = args
        sc = jnp.einsum('bqhcd,bkhcd->bhcqk', qb, k, preferred_element_type=jnp.float32) * scale
        qpos = start + jnp.arange(Q_BLOCK)
        dist = jnp.abs(qpos[:, None] - kpos[None, :]).astype(jnp.float32)
        sc = sc - slopes[:, None, None, None] * dist
        p = jax.nn.softmax(sc, axis=-1)
        a = (p[:, :, 0] - lam * p[:, :, 1]).astype(v.dtype)
        return jnp.einsum('bhqk,bkhe->bqhe', a, v)

    starts = jnp.arange(s // Q_BLOCK) * Q_BLOCK
    o = merge_blocks(lax.map(block, (query_blocks(q), starts)))
    o = rmsnorm(o, subln_gain) * (1.0 - lambda_init)
    return o.reshape(b, s, B_HEADS * 2 * HEAD_DIM) @ w_o


def dilated_group(q, k, v, window, dilation, slopes):
    b, s, nh, hd = q.shape
    radius = window // (2 * dilation)
    n_res = s // dilation
    nblk = (n_res + Q_BLOCK - 1) // Q_BLOCK
    lp = nblk * Q_BLOCK
    kw = Q_BLOCK + 2 * radius

    def to_residue(x):
        x = x.reshape(b, n_res, dilation, nh, hd)
        return jnp.swapaxes(x, 1, 2).reshape(b * dilation, n_res, nh, hd)

    qr = jnp.pad(to_residue(q), ((0, 0), (0, lp - n_res), (0, 0), (0, 0)))
    qr = qr.reshape(b * dilation, nblk, Q_BLOCK, nh, hd)
    key_idx = (jnp.arange(nblk) * Q_BLOCK)[:, None] + jnp.arange(kw)[None, :]
    pad = ((0, 0), (radius, lp - n_res + radius), (0, 0), (0, 0))
    kb = jnp.pad(to_residue(k), pad)[:, key_idx]
    vb = jnp.pad(to_residue(v), pad)[:, key_idx]
    j_abs = key_idx - radius
    rel = jnp.arange(Q_BLOCK)[:, None] - jnp.arange(kw)[None, :] + radius
    valid = (jnp.abs(rel) <= radius)[None] & ((j_abs >= 0) & (j_abs < n_res))[:, None, :]
    sc = jnp.einsum('gnqhd,gnkhd->gnhqk', qr, kb, preferred_element_type=jnp.float32) * (hd ** -0.5)
    sc = sc - slopes[:, None, None] * (dilation * jnp.abs(rel)).astype(jnp.float32)
    sc = jnp.where(valid[None, :, None], sc, MASK_VALUE)
    lse = jax.nn.logsumexp(sc, axis=-1)
    p = jnp.exp(sc - lse[..., None]).astype(v.dtype)
    o = jnp.einsum('gnhqk,gnkhd->gnqhd', p, vb)

    def from_residue(x):
        x = x.reshape((b, dilation, lp) + x.shape[3:])[:, :, :n_res]
        x = jnp.swapaxes(x, 1, 2)
        return x.reshape((b, s) + x.shape[3:])

    return from_residue(o), from_residue(jnp.swapaxes(lse, 2, 3))


def mixer_dilated(h, w_qkv, w_o):
    b, s, _ = h.shape
    qkv = (h @ w_qkv).reshape(b, s, len(C_CONFIGS), 3, C_HEADS, HEAD_DIM)
    slopes = alibi_slopes(C_HEADS)
    outs, lses = [], []
    for g, (window, dilation) in enumerate(C_CONFIGS):
        o, l = dilated_group(qkv[:, :, g, 0], qkv[:, :, g, 1], qkv[:, :, g, 2], window, dilation, slopes)
        outs.append(o)
        lses.append(l)
    wts = jax.nn.softmax(jnp.stack(lses), axis=0)
    o = jnp.sum(wts[..., None] * jnp.stack(outs).astype(jnp.float32), axis=0).astype(h.dtype)
    return o.reshape(b, s, C_HEADS * HEAD_DIM) @ w_o


def mixer_neighbourhood(h, w_qkv, rpb, w_o):
    b, s, _ = h.shape
    rows = s // GRID_W
    kr = min(NA_ROWS, rows)
    qkv = (h @ w_qkv).reshape(b, rows, GRID_W, 3, D_HEADS, HEAD_DIM)
    q, k, v = qkv[:, :, :, 0], qkv[:, :, :, 1], qkv[:, :, :, 2]
    c = jnp.arange(GRID_W)
    cs = jnp.clip(c - NA_COLS // 2, 0, GRID_W - NA_COLS)
    col_ok = (c[None, :] >= cs[:, None]) & (c[None, :] < cs[:, None] + NA_COLS)
    col_idx = jnp.clip(c[None, :] - c[:, None] + NA_COLS - 1, 0, 2 * NA_COLS - 2)
    rpb_c = rpb[:, :, col_idx]
    scale = HEAD_DIM ** -0.5

    def row_step(r):
        rs = jnp.clip(r - kr // 2, 0, rows - kr)
        qr = lax.dynamic_index_in_dim(q, r, axis=1, keepdims=False)
        ks = lax.dynamic_slice_in_dim(k, rs, kr, axis=1)
        vs = lax.dynamic_slice_in_dim(v, rs, kr, axis=1)
        sc = jnp.einsum('bqhd,brkhd->bhqrk', qr, ks, preferred_element_type=jnp.float32) * scale
        row_off = rs + jnp.arange(kr) - r + NA_ROWS - 1
        bias = jnp.transpose(jnp.take(rpb_c, row_off, axis=1), (0, 2, 1, 3))
        sc = jnp.where(col_ok[:, None, :], sc + bias.astype(jnp.float32), MASK_VALUE)
        p = jax.nn.softmax(sc.reshape(b, D_HEADS, GRID_W, kr * GRID_W), axis=-1)
        p = p.reshape(sc.shape).astype(v.dtype)
        return jnp.einsum('bhqrk,brkhd->bqhd', p, vs)

    o = lax.map(row_step, jnp.arange(rows))
    o = jnp.moveaxis(o, 0, 1).reshape(b, s, D_HEADS * HEAD_DIM)
    return o @ w_o


def swiglu(h, w_gate, w_up, w_down):
    return (jax.nn.silu(h @ w_gate) * (h @ w_up)) @ w_down


def trunk(x, norm_mix, norm_ffn, norm_final,
          a_w_qkv, a_q_gain, a_k_gain, a_w_o,
          b_w_qkv, b_lambda_q1, b_lambda_k1, b_lambda_q2, b_lambda_k2, b_subln_gain, b_w_o,
          c_w_qkv, c_w_o, d_w_qkv, d_rpb, d_w_o,
          ffn_w_gate, ffn_w_up, ffn_w_down):
    for i in range(DEPTH):
        h = rmsnorm(x, norm_mix[i])
        kind = i % N_MIXERS
        if kind == 0:
            m = mixer_gqa_axial(h, a_w_qkv, a_q_gain, a_k_gain, a_w_o)
        elif kind == 1:
            m = mixer_differential(h, b_w_qkv, b_lambda_q1, b_lambda_k1, b_lambda_q2, b_lambda_k2,
                                   b_subln_gain, b_w_o, i)
        elif kind == 2:
            m = mixer_dilated(h, c_w_qkv, c_w_o)
        else:
            m = mixer_neighbourhood(h, d_w_qkv, d_rpb, d_w_o)
        x = x + m
        x = x + swiglu(rmsnorm(x, norm_ffn[i]), ffn_w_gate[i], ffn_w_up[i], ffn_w_down[i])
    return rmsnorm(x, norm_final)


def setup_inputs(seed: int = 0) -> dict:
    key = jax.random.key(seed)
    ks = jax.random.split(key, 24)
    f32 = jnp.float32

    def nrm(k, shape, scale):
        return jax.random.normal(k, shape, f32) * scale

    def gain(k, shape):
        return 1.0 + 0.02 * jax.random.normal(k, shape, f32)

    d = D_MODEL
    a_qkv = (A_HEADS + 2 * A_KV_HEADS) * HEAD_DIM
    b_w = B_HEADS * 2 * HEAD_DIM
    c_w = C_HEADS * HEAD_DIM
    d_w = D_HEADS * HEAD_DIM
    return {
        'x_prompt': jax.random.normal(ks[0], (BATCH, SEQ, d), f32),
        'x_sample': jax.random.normal(ks[1], (DEC_BATCH, DEC_SEQ, d), f32),
        'norm_mix': gain(ks[2], (DEPTH, d)),
        'norm_ffn': gain(ks[3], (DEPTH, d)),
        'norm_final': gain(ks[4], (d,)),
        'a_w_qkv': nrm(ks[5], (d, a_qkv), d ** -0.5),
        'a_q_gain': gain(ks[6], (HEAD_DIM,)),
        'a_k_gain': gain(ks[7], (HEAD_DIM,)),
        'a_w_o': nrm(ks[8], (A_HEADS * HEAD_DIM, d), (A_HEADS * HEAD_DIM) ** -0.5),
        'b_w_qkv': nrm(ks[9], (d, 3 * b_w), d ** -0.5),
        'b_lambda_q1': nrm(ks[10], (HEAD_DIM,), 0.1),
        'b_lambda_k1': nrm(ks[11], (HEAD_DIM,), 0.1),
        'b_lambda_q2': nrm(ks[12], (HEAD_DIM,), 0.1),
        'b_lambda_k2': nrm(ks[13], (HEAD_DIM,), 0.1),
        'b_subln_gain': gain(ks[14], (2 * HEAD_DIM,)),
        'b_w_o': nrm(ks[15], (b_w, d), b_w ** -0.5),
        'c_w_qkv': nrm(ks[16], (d, len(C_CONFIGS) * 3 * c_w), d ** -0.5),
        'c_w_o': nrm(ks[17], (c_w, d), c_w ** -0.5),
        'd_w_qkv': nrm(ks[18], (d, 3 * d_w), d ** -0.5),
        'd_rpb': nrm(ks[19], (D_HEADS, 2 * NA_ROWS - 1, 2 * NA_COLS - 1), 0.02),
        'd_w_o': nrm(ks[20], (d_w, d), d_w ** -0.5),
        'ffn_w_gate': nrm(ks[21], (DEPTH, d, D_FF), d ** -0.5),
        'ffn_w_up': nrm(ks[22], (DEPTH, d, D_FF), d ** -0.5),
        'ffn_w_down': nrm(ks[23], (DEPTH, D_FF, d), D_FF ** -0.5),
    }


def reference(x_prompt, x_sample, norm_mix, norm_ffn, norm_final,
              a_w_qkv, a_q_gain, a_k_gain, a_w_o,
              b_w_qkv, b_lambda_q1, b_lambda_k1, b_lambda_q2, b_lambda_k2, b_subln_gain, b_w_o,
              c_w_qkv, c_w_o, d_w_qkv, d_rpb, d_w_o,
              ffn_w_gate, ffn_w_up, ffn_w_down):
    def run(x):
        return trunk(x, norm_mix, norm_ffn, norm_final,
                     a_w_qkv, a_q_gain, a_k_gain, a_w_o,
                     b_w_qkv, b_lambda_q1, b_lambda_k1, b_lambda_q2, b_lambda_k2, b_subln_gain, b_w_o,
                     c_w_qkv, c_w_o, d_w_qkv, d_rpb, d_w_o,
                     ffn_w_gate, ffn_w_up, ffn_w_down)

    y_prompt = run(x_prompt)
    y_sample = run(x_sample)
    return (y_prompt, y_sample)
```

```python
import functools
import math

import jax
import jax.numpy as jnp
from jax import lax
from jax.experimental import pallas as pl
from jax.experimental.pallas import tpu as pltpu

HEAD_DIM = 128
GRID_W = 64
Q_BLOCK = 128
RMS_EPS = 1e-6
MASK_VALUE = -1e30
ROPE_THETA = 10000.0
C_CONFIGS = ((128, 1), (512, 4), (2048, 16))
NA_ROWS = 8
NA_COLS = 16
LANES = 128
V7X_VMEM_LIMIT_BYTES = 56 * 1024 * 1024
FF_ALIGN = 1024

F32 = jnp.float32
BF16 = jnp.bfloat16
NT_DIMS = (((1,), (1,)), ((), ()))


def _params(*sem):
    return pltpu.CompilerParams(dimension_semantics=sem, vmem_limit_bytes=V7X_VMEM_LIMIT_BYTES)


def _pick(n, pref, align=LANES):
    if n <= pref:
        return n
    b = (pref // align) * align
    while b >= align:
        if n % b == 0:
            return b
        b -= align
    raise ValueError(f"no block for {n} under {pref}")


def _rmsnorm_body(x_ref, g_ref, o_ref):
    x = x_ref[...]
    ms = jnp.mean(x * x, axis=-1, keepdims=True)
    o_ref[...] = (x * lax.rsqrt(ms + RMS_EPS) * g_ref[...]).astype(o_ref.dtype)


def rmsnorm(x, gain, out_dtype, row0=0, rows=None):
    t, d = x.shape
    rows = t if rows is None else rows
    bm = _pick(math.gcd(rows, row0) if row0 else rows, 256, 8)
    off = row0 // bm
    return pl.pallas_call(
        _rmsnorm_body,
        grid=(rows // bm,),
        in_specs=[pl.BlockSpec((bm, d), lambda i: (i + off, 0)),
                  pl.BlockSpec((1, d), lambda i: (0, 0))],
        out_specs=pl.BlockSpec((bm, d), lambda i: (i, 0)),
        out_shape=jax.ShapeDtypeStruct((rows, d), out_dtype),
        compiler_params=_params("parallel"),
        name="rmsnorm",
    )(x, gain.reshape(1, d).astype(F32))


def _mm_body(a_ref, b_ref, o_ref):
    o_ref[...] = jnp.dot(a_ref[...], b_ref[...], preferred_element_type=F32).astype(o_ref.dtype)


def _mm_res_body(a_ref, b_ref, r_ref, o_ref):
    acc = jnp.dot(a_ref[...], b_ref[...], preferred_element_type=F32)
    o_ref[...] = (acc + r_ref[...]).astype(o_ref.dtype)


def matmul(a, b, out_dtype, residual=None, bm_pref=1024, bn_pref=512):
    m, k = a.shape
    _, n = b.shape
    bm = _pick(m, bm_pref, 8)
    bn = _pick(n, bn_pref)
    in_specs = [pl.BlockSpec((bm, k), lambda i, j: (i, 0)),
                pl.BlockSpec((k, bn), lambda i, j: (0, j))]
    args = [a, b]
    body = _mm_body
    if residual is not None:
        in_specs.append(pl.BlockSpec((bm, bn), lambda i, j: (i, j)))
        args.append(residual)
        body = _mm_res_body
    return pl.pallas_call(
        body,
        grid=(m // bm, n // bn),
        in_specs=in_specs,
        out_specs=pl.BlockSpec((bm, bn), lambda i, j: (i, j)),
        out_shape=jax.ShapeDtypeStruct((m, n), out_dtype),
        compiler_params=_params("parallel", "parallel"),
        name="matmul",
    )(*args)


def _mm_kgrid_res_body(a_ref, b_ref, r_ref, o_ref, acc_ref):
    kk = pl.program_id(2)

    @pl.when(kk == 0)
    def _():
        acc_ref[...] = r_ref[...]

    acc_ref[...] += jnp.dot(a_ref[...], b_ref[...], preferred_element_type=F32)

    @pl.when(kk == pl.num_programs(2) - 1)
    def _():
        o_ref[...] = acc_ref[...]


def matmul_kgrid_res(a, b, residual, bm_pref=1024, bn_pref=1024, bk_pref=1024):
    m, k = a.shape
    _, n = b.shape
    bm, bn, bk = _pick(m, bm_pref, 8), _pick(n, bn_pref), _pick(k, bk_pref)
    return pl.pallas_call(
        _mm_kgrid_res_body,
        grid=(m // bm, n // bn, k // bk),
        in_specs=[pl.BlockSpec((bm, bk), lambda i, j, kk: (i, kk)),
                  pl.BlockSpec((bk, bn), lambda i, j, kk: (kk, j)),
                  pl.BlockSpec((bm, bn), lambda i, j, kk: (i, j))],
        out_specs=pl.BlockSpec((bm, bn), lambda i, j, kk: (i, j)),
        out_shape=jax.ShapeDtypeStruct((m, n), F32),
        scratch_shapes=[pltpu.VMEM((bm, bn), F32)],
        compiler_params=_params("parallel", "parallel", "arbitrary"),
        name="matmul_kgrid",
    )(a, b, residual)


def _gateup_body(h_ref, wg_ref, wu_ref, o_ref):
    h = h_ref[...]
    g = jnp.dot(h, wg_ref[...], preferred_element_type=F32)
    u = jnp.dot(h, wu_ref[...], preferred_element_type=F32)
    o_ref[...] = (g / (1.0 + jnp.exp(-g)) * u).astype(o_ref.dtype)


def gateup(h, wg, wu, bm_pref=1024, bn_pref=512):
    m, k = h.shape
    _, n = wg.shape
    bm, bn = _pick(m, bm_pref, 8), _pick(n, bn_pref)
    return pl.pallas_call(
        _gateup_body,
        grid=(m // bm, n // bn),
        in_specs=[pl.BlockSpec((bm, k), lambda i, j: (i, 0)),
                  pl.BlockSpec((k, bn), lambda i, j: (0, j)),
                  pl.BlockSpec((k, bn), lambda i, j: (0, j))],
        out_specs=pl.BlockSpec((bm, bn), lambda i, j: (i, j)),
        out_shape=jax.ShapeDtypeStruct((m, n), BF16),
        compiler_params=_params("parallel", "parallel"),
        name="gateup",
    )(h, wg, wu)


class Seqs:
    def __init__(self, b1, s1, b2, s2):
        assert b1 == 1 and s1 == 2 * s2, "layout assumes one prompt of twice the sample length"
        self.s1, self.s2, self.nb2 = s1, s2, b2
        self.t = s1 + b2 * s2

    def positions(self):
        return jnp.concatenate([jnp.arange(self.s1)] + [jnp.arange(self.s2)] * self.nb2)


def _softmax_step(s, m, l):
    m_new = jnp.maximum(m, jnp.max(s, axis=-1, keepdims=True))
    alpha = jnp.exp(m - m_new)
    p = jnp.exp(s - m_new)
    return p, alpha, m_new, alpha * l + jnp.sum(p, axis=-1, keepdims=True)


def _swap_quarters(y):
    lane = lax.broadcasted_iota(jnp.int32, y.shape, 1)
    first = (lane % (HEAD_DIM // 2)) < (HEAD_DIM // 4)
    return jnp.where(first, pltpu.roll(y, HEAD_DIM - HEAD_DIM // 4, 1), pltpu.roll(y, HEAD_DIM // 4, 1))


def _rope_body(x_ref, cos_ref, sin_ref, qg_ref, kg_ref, q_ref, k_ref, v_ref, *, nq, nk, scale):
    cos, sin = cos_ref[...], sin_ref[...]

    def norm_rope(x, gain):
        ms = jnp.mean(x * x, axis=-1, keepdims=True)
        y = x * lax.rsqrt(ms + RMS_EPS) * gain
        return y * cos + _swap_quarters(y) * sin

    for h in range(nq):
        sl = slice(h * HEAD_DIM, (h + 1) * HEAD_DIM)
        q_ref[:, sl] = (norm_rope(x_ref[:, sl], qg_ref[...]) * scale).astype(q_ref.dtype)
    for h in range(nk):
        src = slice((nq + h) * HEAD_DIM, (nq + h + 1) * HEAD_DIM)
        k_ref[:, h * HEAD_DIM:(h + 1) * HEAD_DIM] = norm_rope(x_ref[:, src], kg_ref[...]).astype(k_ref.dtype)
    v_ref[...] = x_ref[:, (nq + nk) * HEAD_DIM:].astype(v_ref.dtype)


def rope_qk(qkv, cos, sin, q_gain, k_gain, nq, nk):
    t, w = qkv.shape
    bm = _pick(t, 256, 8)
    row = lambda i: (i, 0)
    fixed = lambda i: (0, 0)
    return pl.pallas_call(
        functools.partial(_rope_body, nq=nq, nk=nk, scale=HEAD_DIM ** -0.5),
        grid=(t // bm,),
        in_specs=[pl.BlockSpec((bm, w), row), pl.BlockSpec((bm, HEAD_DIM), row), pl.BlockSpec((bm, HEAD_DIM), row),
                  pl.BlockSpec((1, HEAD_DIM), fixed), pl.BlockSpec((1, HEAD_DIM), fixed)],
        out_specs=[pl.BlockSpec((bm, nq * HEAD_DIM), row), pl.BlockSpec((bm, nk * HEAD_DIM), row),
                   pl.BlockSpec((bm, nk * HEAD_DIM), row)],
        out_shape=[jax.ShapeDtypeStruct((t, nq * HEAD_DIM), BF16), jax.ShapeDtypeStruct((t, nk * HEAD_DIM), BF16),
                   jax.ShapeDtypeStruct((t, nk * HEAD_DIM), BF16)],
        compiler_params=_params("parallel"),
        name="rope_qk",
    )(qkv, cos, sin, q_gain.reshape(1, -1).astype(F32), k_gain.reshape(1, -1).astype(F32))


def rope_tables(seqs):
    half = HEAD_DIM // 2
    inv = ROPE_THETA ** (-jnp.arange(0, half, 2, dtype=F32) / half)
    t = seqs.positions()
    ang_r = (t // GRID_W).astype(F32)[:, None] * inv
    ang_c = (t % GRID_W).astype(F32)[:, None] * inv
    cr, sr, cc, sc = jnp.cos(ang_r), jnp.sin(ang_r), jnp.cos(ang_c), jnp.sin(ang_c)
    return jnp.concatenate([cr, cr, cc, cc], axis=-1), jnp.concatenate([-sr, sr, -sc, sc], axis=-1)


def _kv_block_maps(seqs, bq):
    npq = seqs.s1 // bq
    per = seqs.s2 // bq

    def lo(i):
        return jnp.where(i < npq, 0, 2 + (i - npq) // per)

    def hi(i):
        return jnp.where(i < npq, 1, 2 + (i - npq) // per)

    return npq, lo, hi


def _gqa_body(q_ref, klo_ref, khi_ref, vlo_ref, vhi_ref, o_ref, *, rep, bq, bkv, nhalf, npq):
    i = pl.program_id(1)
    q = q_ref[...]
    qs = jnp.concatenate([q[:, r * HEAD_DIM:(r + 1) * HEAD_DIM] for r in range(rep)], axis=0)
    m_rows = rep * bq

    def make_step(k_ref, v_ref):
        def step(j, carry):
            m, l, acc = carry
            k = k_ref[pl.ds(pl.multiple_of(j * bkv, bkv), bkv), :]
            v = v_ref[pl.ds(pl.multiple_of(j * bkv, bkv), bkv), :]
            s = lax.dot_general(qs, k, NT_DIMS, preferred_element_type=F32)
            p, alpha, m, l = _softmax_step(s, m, l)
            acc = alpha * acc + jnp.dot(p.astype(BF16), v, preferred_element_type=F32)
            return m, l, acc
        return step

    carry = (jnp.full((m_rows, 1), MASK_VALUE, F32), jnp.zeros((m_rows, 1), F32), jnp.zeros((m_rows, HEAD_DIM), F32))
    carry = lax.fori_loop(0, nhalf, make_step(klo_ref, vlo_ref), carry)
    carry = lax.fori_loop(0, jnp.where(i < npq, nhalf, 0), make_step(khi_ref, vhi_ref), carry)
    _, l, acc = carry
    o = acc / l
    for r in range(rep):
        o_ref[:, r * HEAD_DIM:(r + 1) * HEAD_DIM] = o[r * bq:(r + 1) * bq].astype(o_ref.dtype)


def gqa_attention(q, k, v, seqs, bq_pref=256, bkv_pref=512):
    t, wq = q.shape
    nk = k.shape[1] // HEAD_DIM
    rep = wq // HEAD_DIM // nk
    bq = _pick(seqs.s2, bq_pref, 16)
    bkv = _pick(seqs.s2, bkv_pref)
    npq, lo, hi = _kv_block_maps(seqs, bq)
    kv_spec = lambda f: pl.BlockSpec((seqs.s2, HEAD_DIM), lambda g, i: (f(i), g))
    return pl.pallas_call(
        functools.partial(_gqa_body, rep=rep, bq=bq, bkv=bkv, nhalf=seqs.s2 // bkv, npq=npq),
        grid=(nk, t // bq),
        in_specs=[pl.BlockSpec((bq, rep * HEAD_DIM), lambda g, i: (i, g)),
                  kv_spec(lo), kv_spec(hi), kv_spec(lo), kv_spec(hi)],
        out_specs=pl.BlockSpec((bq, rep * HEAD_DIM), lambda g, i: (i, g)),
        out_shape=jax.ShapeDtypeStruct((t, wq), BF16),
        compiler_params=_params("parallel", "arbitrary"),
        name="gqa_attention",
    )(q, k, k, v, v)


def _diff_body(q_ref, klo_ref, khi_ref, vlo_ref, vhi_ref, slope_ref, lq1_ref, lk1_ref, lq2_ref, lk2_ref, g_ref, o_ref,
               *, bq, bkv, nhalf, npq, per, s2, lambda_init):
    i = pl.program_id(1)
    q = q_ref[...]
    q0, q1 = q[:, :HEAD_DIM], q[:, HEAD_DIM:]
    slope = slope_ref[0][:, :1]
    qpos0 = jnp.where(i < npq, i, (i - npq) % per) * bq
    rel = (lax.broadcasted_iota(jnp.int32, (bq, bkv), 0) - lax.broadcasted_iota(jnp.int32, (bq, bkv), 1))

    def make_step(k_ref, v_ref, kbase):
        def step(j, carry):
            m, l, acc = carry
            k = k_ref[pl.ds(pl.multiple_of(j * bkv, bkv), bkv), :]
            v = v_ref[pl.ds(pl.multiple_of(j * bkv, bkv), bkv), :]
            s0 = lax.dot_general(q0, k[:, :HEAD_DIM], NT_DIMS, preferred_element_type=F32)
            s1 = lax.dot_general(q1, k[:, HEAD_DIM:], NT_DIMS, preferred_element_type=F32)
            dist = jnp.abs(rel + (qpos0 - kbase - j * bkv)).astype(F32)
            bias = slope * dist
            s = jnp.concatenate([s0 - bias, s1 - bias], axis=0)
            p, alpha, m, l = _softmax_step(s, m, l)
            acc = alpha * acc + jnp.dot(p.astype(BF16), v, preferred_element_type=F32)
            return m, l, acc
        return step

    carry = (jnp.full((2 * bq, 1), MASK_VALUE, F32), jnp.zeros((2 * bq, 1), F32), jnp.zeros((2 * bq, 2 * HEAD_DIM), F32))
    carry = lax.fori_loop(0, nhalf, make_step(klo_ref, vlo_ref, 0), carry)
    carry = lax.fori_loop(0, jnp.where(i < npq, nhalf, 0), make_step(khi_ref, vhi_ref, s2), carry)
    _, l, acc = carry
    o = acc / l
    lam = (jnp.exp(jnp.sum(lq1_ref[...] * lk1_ref[...], axis=-1, keepdims=True))
           - jnp.exp(jnp.sum(lq2_ref[...] * lk2_ref[...], axis=-1, keepdims=True)) + lambda_init)
    d = o[:bq] - lam * o[bq:]
    ms = jnp.mean(d * d, axis=-1, keepdims=True)
    o_ref[...] = (d * lax.rsqrt(ms + RMS_EPS) * g_ref[...] * (1.0 - lambda_init)).astype(o_ref.dtype)


def diff_attention(qkv, slopes, lq1, lk1, lq2, lk2, subln_gain, seqs, lambda_init, bq_pref=256, bkv_pref=512):
    t, w = qkv.shape
    hw = 2 * HEAD_DIM
    nh = w // (3 * hw)
    bq = _pick(seqs.s2, bq_pref, 16)
    bkv = _pick(seqs.s2, bkv_pref)
    npq, lo, hi = _kv_block_maps(seqs, bq)
    kv_spec = lambda f, base: pl.BlockSpec((seqs.s2, hw), lambda h, i: (f(i), base + h))
    vec = lambda a: a.reshape(1, -1).astype(F32)
    vec_spec = lambda n: pl.BlockSpec((1, n), lambda h, i: (0, 0))
    return pl.pallas_call(
        functools.partial(_diff_body, bq=bq, bkv=bkv, nhalf=seqs.s2 // bkv, npq=npq, per=seqs.s2 // bq, s2=seqs.s2,
                          lambda_init=lambda_init),
        grid=(nh, t // bq),
        in_specs=[pl.BlockSpec((bq, hw), lambda h, i: (i, h)),
                  kv_spec(lo, nh), kv_spec(hi, nh), kv_spec(lo, 2 * nh), kv_spec(hi, 2 * nh),
                  pl.BlockSpec((1, 1, LANES), lambda h, i: (h, 0, 0)),
                  vec_spec(HEAD_DIM), vec_spec(HEAD_DIM), vec_spec(HEAD_DIM), vec_spec(HEAD_DIM), vec_spec(hw)],
        out_specs=pl.BlockSpec((bq, hw), lambda h, i: (i, h)),
        out_shape=jax.ShapeDtypeStruct((t, nh * hw), BF16),
        compiler_params=_params("parallel", "arbitrary"),
        name="diff_attention",
    )(qkv, qkv, qkv, qkv, qkv, jnp.broadcast_to(slopes.astype(F32)[:, None, None], (nh, 1, LANES)),
      vec(lq1), vec(lk1), vec(lq2), vec(lk2), vec(subln_gain))


def _dilated_body(q_ref, kp_ref, kc_ref, kn_ref, vp_ref, vc_ref, vn_ref, o_ref, lse_ref, *, nh, dilation, radius,
                  rows_p, rows_s):
    n = pl.program_id(1)
    qb = Q_BLOCK
    row0 = n * qb
    seq_start = jnp.where(row0 < rows_p, 0, rows_p + (row0 - rows_p) // rows_s * rows_s)
    seq_end = seq_start + jnp.where(row0 < rows_p, rows_p, rows_s)
    kw = qb + 2 * radius
    r_i = lax.broadcasted_iota(jnp.int32, (qb, kw), 0)
    c_i = lax.broadcasted_iota(jnp.int32, (qb, kw), 1)
    jrel = c_i - radius - r_i
    kabs = row0 - radius + c_i
    valid = (jnp.abs(jrel) <= radius) & (kabs >= seq_start) & (kabs < seq_end)
    dist = (dilation * jnp.abs(jrel)).astype(F32)
    lane = lax.broadcasted_iota(jnp.int32, (qb, LANES), 1)
    lse_tile = jnp.zeros((qb, LANES), F32)
    for h in range(nh):
        sl = slice(h * HEAD_DIM, (h + 1) * HEAD_DIM)
        slope = 2.0 ** (-8.0 * (h + 1) / nh)
        k = jnp.concatenate([kp_ref[qb - radius:, sl], kc_ref[:, sl], kn_ref[:radius, sl]], axis=0)
        v = jnp.concatenate([vp_ref[qb - radius:, sl], vc_ref[:, sl], vn_ref[:radius, sl]], axis=0)
        s = lax.dot_general(q_ref[:, sl], k, NT_DIMS, preferred_element_type=F32)
        s = jnp.where(valid, s - slope * dist, MASK_VALUE)
        m = jnp.max(s, axis=-1, keepdims=True)
        p = jnp.exp(s - m)
        l = jnp.sum(p, axis=-1, keepdims=True)
        o_ref[:, sl] = jnp.dot(p.astype(BF16), v, preferred_element_type=F32) / l
        lse_tile = jnp.where(lane == h, m + jnp.log(l), lse_tile)
    lse_ref[...] = lse_tile


def dilated_group(qkv, g, ngroups, nh, window, dilation, seqs):
    t, w = qkv.shape
    hw = nh * HEAD_DIM
    radius = window // (2 * dilation)
    assert radius <= Q_BLOCK and seqs.s2 % (dilation * Q_BLOCK) == 0
    rows = t // dilation
    nblk = rows // Q_BLOCK
    view = qkv.reshape(rows, dilation * w)
    wb = w // hw

    def spec(which, shift):
        return pl.BlockSpec((Q_BLOCK, hw),
                            lambda c, n: (jnp.clip(n + shift, 0, nblk - 1), c * wb + g * 3 + which))

    o, lse = pl.pallas_call(
        functools.partial(_dilated_body, nh=nh, dilation=dilation, radius=radius,
                          rows_p=seqs.s1 // dilation, rows_s=seqs.s2 // dilation),
        grid=(dilation, nblk),
        in_specs=[spec(0, 0), spec(1, -1), spec(1, 0), spec(1, 1), spec(2, -1), spec(2, 0), spec(2, 1)],
        out_specs=[pl.BlockSpec((Q_BLOCK, hw), lambda c, n: (n, c)),
                   pl.BlockSpec((Q_BLOCK, LANES), lambda c, n: (n, c))],
        out_shape=[jax.ShapeDtypeStruct((rows, dilation * hw), F32),
                   jax.ShapeDtypeStruct((rows, dilation * LANES), F32)],
        compiler_params=_params("parallel", "parallel"),
        name=f"dilated_d{dilation}",
    )(view, view, view, view, view, view, view)
    return o.reshape(t, hw), lse.reshape(t, LANES)


def _merge_body(*refs, ng, nh):
    o_refs, lse_refs, out_ref = refs[:ng], refs[ng:2 * ng], refs[2 * ng]
    lses = [r[...] for r in lse_refs]
    mx = functools.reduce(jnp.maximum, lses)
    es = [jnp.exp(x - mx) for x in lses]
    tot = functools.reduce(lambda a, b: a + b, es)
    ws = [e / tot for e in es]
    for h in range(nh):
        sl = slice(h * HEAD_DIM, (h + 1) * HEAD_DIM)
        acc = ws[0][:, h:h + 1] * o_refs[0][:, sl]
        for gi in range(1, ng):
            acc = acc + ws[gi][:, h:h + 1] * o_refs[gi][:, sl]
        out_ref[:, sl] = acc.astype(out_ref.dtype)


def merge_groups(outs, lses, nh):
    t, hw = outs[0].shape
    ng = len(outs)
    bm = _pick(t, 256, 8)
    return pl.pallas_call(
        functools.partial(_merge_body, ng=ng, nh=nh),
        grid=(t // bm,),
        in_specs=[pl.BlockSpec((bm, hw), lambda i: (i, 0))] * ng + [pl.BlockSpec((bm, LANES), lambda i: (i, 0))] * ng,
        out_specs=pl.BlockSpec((bm, hw), lambda i: (i, 0)),
        out_shape=jax.ShapeDtypeStruct((t, hw), BF16),
        compiler_params=_params("parallel"),
        name="dilated_merge",
    )(*outs, *lses)


def _na_row_maps(seqs):
    rows_p, rows_s = seqs.s1 // GRID_W, seqs.s2 // GRID_W

    def local(r):
        is_p = r < rows_p
        start = jnp.where(is_p, 0, rows_p + (r - rows_p) // rows_s * rows_s)
        nrows = jnp.where(is_p, rows_p, rows_s)
        rl = r - start
        rs = jnp.clip(rl - NA_ROWS // 2, 0, nrows - NA_ROWS)
        return start + rs, rl - NA_ROWS // 2 - rs

    return local


def _na_body(q_ref, k_ref, v_ref, b_ref, o_ref, *, nh):
    for h in range(nh):
        sl = slice(h * HEAD_DIM, (h + 1) * HEAD_DIM)
        s = lax.dot_general(q_ref[:, sl], k_ref[:, sl], NT_DIMS, preferred_element_type=F32) + b_ref[0, h]
        m = jnp.max(s, axis=-1, keepdims=True)
        p = jnp.exp(s - m)
        l = jnp.sum(p, axis=-1, keepdims=True)
        o_ref[:, sl] = (jnp.dot(p.astype(BF16), v_ref[:, sl], preferred_element_type=F32) / l).astype(o_ref.dtype)


def na_bias_tables(rpb):
    c = jnp.arange(GRID_W)
    cs = jnp.clip(c - NA_COLS // 2, 0, GRID_W - NA_COLS)
    col_ok = (c[None, :] >= cs[:, None]) & (c[None, :] < cs[:, None] + NA_COLS)
    col_idx = jnp.clip(c[None, :] - c[:, None] + NA_COLS - 1, 0, 2 * NA_COLS - 2)
    rpb_c = jnp.where(col_ok[None, None], rpb.astype(F32)[:, :, col_idx], MASK_VALUE)
    tabs = []
    for e in range(-NA_ROWS // 2, NA_ROWS // 2):
        off = jnp.arange(NA_ROWS) + NA_ROWS // 2 - 1 - e
        tabs.append(jnp.transpose(rpb_c[:, off], (0, 2, 1, 3)).reshape(rpb.shape[0], GRID_W, NA_ROWS * GRID_W))
    return jnp.stack(tabs)


def neighbourhood_attention(qkv, bias, nh, seqs):
    t, w = qkv.shape
    hw = nh * HEAD_DIM
    local = _na_row_maps(seqs)
    kw = NA_ROWS * GRID_W
    return pl.pallas_call(
        functools.partial(_na_body, nh=nh),
        grid=(t // GRID_W,),
        in_specs=[pl.BlockSpec((GRID_W, hw), lambda r: (r, 0)),
                  pl.BlockSpec((pl.Element(kw), pl.Element(hw)), lambda r: (local(r)[0] * GRID_W, hw)),
                  pl.BlockSpec((pl.Element(kw), pl.Element(hw)), lambda r: (local(r)[0] * GRID_W, 2 * hw)),
                  pl.BlockSpec((1, nh, GRID_W, kw), lambda r: (local(r)[1] + NA_ROWS // 2, 0, 0, 0))],
        out_specs=pl.BlockSpec((GRID_W, hw), lambda r: (r, 0)),
        out_shape=jax.ShapeDtypeStruct((t, hw), BF16),
        compiler_params=_params("parallel"),
        name="neighbourhood_attention",
    )(qkv, qkv, qkv, bias)


def _scale_q_cols(w, widths):
    parts, c0 = [], 0
    for is_q, wd in widths:
        blk = w[:, c0:c0 + wd]
        parts.append(blk * (HEAD_DIM ** -0.5) if is_q else blk)
        c0 += wd
    return jnp.concatenate(parts, axis=1)


def kernel(x_prompt, x_sample, norm_mix, norm_ffn, norm_final, a_w_qkv, a_q_gain, a_k_gain, a_w_o, b_w_qkv,
           b_lambda_q1, b_lambda_k1, b_lambda_q2, b_lambda_k2, b_subln_gain, b_w_o, c_w_qkv, c_w_o, d_w_qkv, d_rpb,
           d_w_o, ffn_w_gate, ffn_w_up, ffn_w_down):
    b1, s1, d = x_prompt.shape
    b2, s2, _ = x_sample.shape
    seqs = Seqs(b1, s1, b2, s2)
    t = seqs.t
    depth = norm_mix.shape[0]
    x = jnp.concatenate([x_prompt.reshape(b1 * s1, d), x_sample.reshape(b2 * s2, d)], axis=0)

    a_nk = (a_w_qkv.shape[1] - d) // (2 * HEAD_DIM)
    a_nq = d // HEAD_DIM
    b_nh = b_w_o.shape[0] // (2 * HEAD_DIM)
    c_nh = c_w_o.shape[0] // HEAD_DIM
    d_nh = d_w_o.shape[0] // HEAD_DIM
    ng = len(C_CONFIGS)

    wa = a_w_qkv.astype(BF16)
    wb = _scale_q_cols(b_w_qkv, [(True, b_nh * 2 * HEAD_DIM), (False, 2 * b_nh * 2 * HEAD_DIM)]).astype(BF16)
    chw = c_nh * HEAD_DIM
    wc = _scale_q_cols(c_w_qkv, [(True, chw), (False, 2 * chw)] * ng).astype(BF16)
    wd = _scale_q_cols(d_w_qkv, [(True, d_nh * HEAD_DIM), (False, 2 * d_nh * HEAD_DIM)]).astype(BF16)
    wo = {0: a_w_o.astype(BF16), 1: b_w_o.astype(BF16), 2: c_w_o.astype(BF16), 3: d_w_o.astype(BF16)}
    dff = ffn_w_gate.shape[2]
    pad = (-dff) % FF_ALIGN
    w_gate = jnp.pad(ffn_w_gate.astype(BF16), ((0, 0), (0, 0), (0, pad)))
    w_up = jnp.pad(ffn_w_up.astype(BF16), ((0, 0), (0, 0), (0, pad)))
    w_down = jnp.pad(ffn_w_down.astype(BF16), ((0, 0), (0, pad), (0, 0)))

    cos, sin = rope_tables(seqs)
    slopes_b = 2.0 ** (-8.0 * jnp.arange(1, b_nh + 1, dtype=F32) / b_nh)
    na_bias = na_bias_tables(d_rpb)

    for i in range(depth):
        h = rmsnorm(x, norm_mix[i], BF16)
        kind = i % 4
        if kind == 0:
            qkv = matmul(h, wa, F32)
            q, k, v = rope_qk(qkv, cos, sin, a_q_gain, a_k_gain, a_nq, a_nk)
            o = gqa_attention(q, k, v, seqs)
        elif kind == 1:
            qkv = matmul(h, wb, BF16)
            lambda_init = 0.8 - 0.6 * math.exp(-0.3 * i)
            o = diff_attention(qkv, slopes_b, b_lambda_q1, b_lambda_k1, b_lambda_q2, b_lambda_k2, b_subln_gain, seqs,
                               lambda_init)
        elif kind == 2:
            qkv = matmul(h, wc, BF16)
            outs, lses = zip(*[dilated_group(qkv, g, ng, c_nh, win, dil, seqs) for g, (win, dil) in enumerate(C_CONFIGS)])
            o = merge_groups(outs, lses, c_nh)
        else:
            qkv = matmul(h, wd, BF16)
            o = neighbourhood_attention(qkv, na_bias, d_nh, seqs)
        x = matmul(o, wo[kind], F32, residual=x)
        h = rmsnorm(x, norm_ffn[i], BF16)
        a = gateup(h, w_gate[i], w_up[i])
        x = matmul_kgrid_res(a, w_down[i], x)

    y_prompt = rmsnorm(x, norm_final, F32, row0=0, rows=b1 * s1).reshape(b1, s1, d)
    y_sample = rmsnorm(x, norm_final, F32, row0=b1 * s1, rows=b2 * s2).reshape(b2, s2, d)
    return y_prompt, y_sample
```

```python
import functools
import math

import jax
import jax.numpy as jnp
import numpy as np
from jax import lax
from jax.experimental import pallas as pl
from jax.experimental.pallas import tpu as pltpu

HEAD_DIM = 128
GRID_W = 64
Q_BLOCK = 128
RMS_EPS = 1e-6
MASK_VALUE = -1e30
ROPE_THETA = 10000.0
C_CONFIGS = ((128, 1), (512, 4), (2048, 16))
NA_ROWS = 8
NA_COLS = 16
NA_BLOCK_ROWS = 4
NA_WIN_ROWS = NA_BLOCK_ROWS + NA_ROWS
NA_HEADS_PER_STEP = 8
LOG2E = math.log2(math.e)
LANES = 128
V7X_VMEM_LIMIT_BYTES = 56 * 1024 * 1024
FF_ALIGN = 1024

F32 = jnp.float32
BF16 = jnp.bfloat16
NT_DIMS = (((1,), (1,)), ((), ()))


def _params(*sem):
    return pltpu.CompilerParams(dimension_semantics=sem, vmem_limit_bytes=V7X_VMEM_LIMIT_BYTES)


def _pick(n, pref, align=LANES):
    if n <= pref:
        return n
    b = (pref // align) * align
    while b >= align:
        if n % b == 0:
            return b
        b -= align
    raise ValueError(f"no block for {n} under {pref}")


def _rmsnorm_body(x_ref, g_ref, o_ref):
    x = x_ref[...]
    ms = jnp.mean(x * x, axis=-1, keepdims=True)
    o_ref[...] = (x * lax.rsqrt(ms + RMS_EPS) * g_ref[...]).astype(o_ref.dtype)


def rmsnorm(x, gain, out_dtype, row0=0, rows=None):
    t, d = x.shape
    rows = t if rows is None else rows
    bm = _pick(math.gcd(rows, row0) if row0 else rows, 256, 8)
    off = row0 // bm
    return pl.pallas_call(
        _rmsnorm_body,
        grid=(rows // bm,),
        in_specs=[pl.BlockSpec((bm, d), lambda i: (i + off, 0)),
                  pl.BlockSpec((1, d), lambda i: (0, 0))],
        out_specs=pl.BlockSpec((bm, d), lambda i: (i, 0)),
        out_shape=jax.ShapeDtypeStruct((rows, d), out_dtype),
        compiler_params=_params("parallel"),
        name="rmsnorm",
    )(x, gain.reshape(1, d).astype(F32))


def _mm_body(a_ref, b_ref, o_ref):
    o_ref[...] = jnp.dot(a_ref[...], b_ref[...], preferred_element_type=F32).astype(o_ref.dtype)


def _mm_res_body(a_ref, b_ref, r_ref, o_ref):
    acc = jnp.dot(a_ref[...], b_ref[...], preferred_element_type=F32)
    o_ref[...] = (acc + r_ref[...]).astype(o_ref.dtype)


def matmul(a, b, out_dtype, residual=None, bm_pref=1024, bn_pref=512):
    m, k = a.shape
    _, n = b.shape
    bm = _pick(m, bm_pref, 8)
    bn = _pick(n, bn_pref)
    in_specs = [pl.BlockSpec((bm, k), lambda i, j: (i, 0)),
                pl.BlockSpec((k, bn), lambda i, j: (0, j))]
    args = [a, b]
    body = _mm_body
    if residual is not None:
        in_specs.append(pl.BlockSpec((bm, bn), lambda i, j: (i, j)))
        args.append(residual)
        body = _mm_res_body
    return pl.pallas_call(
        body,
        grid=(m // bm, n // bn),
        in_specs=in_specs,
        out_specs=pl.BlockSpec((bm, bn), lambda i, j: (i, j)),
        out_shape=jax.ShapeDtypeStruct((m, n), out_dtype),
        compiler_params=_params("parallel", "parallel"),
        name="matmul",
    )(*args)


def _mm_kgrid_res_body(a_ref, b_ref, r_ref, o_ref, acc_ref):
    kk = pl.program_id(2)

    @pl.when(kk == 0)
    def _():
        acc_ref[...] = r_ref[...]

    acc_ref[...] += jnp.dot(a_ref[...], b_ref[...], preferred_element_type=F32)

    @pl.when(kk == pl.num_programs(2) - 1)
    def _():
        o_ref[...] = acc_ref[...]


def matmul_kgrid_res(a, b, residual, bm_pref=1024, bn_pref=1024, bk_pref=2816):
    m, k = a.shape
    _, n = b.shape
    bm, bn, bk = _pick(m, bm_pref, 8), _pick(n, bn_pref), _pick(k, bk_pref)
    return pl.pallas_call(
        _mm_kgrid_res_body,
        grid=(m // bm, n // bn, k // bk),
        in_specs=[pl.BlockSpec((bm, bk), lambda i, j, kk: (i, kk)),
                  pl.BlockSpec((bk, bn), lambda i, j, kk: (kk, j)),
                  pl.BlockSpec((bm, bn), lambda i, j, kk: (i, j))],
        out_specs=pl.BlockSpec((bm, bn), lambda i, j, kk: (i, j)),
        out_shape=jax.ShapeDtypeStruct((m, n), F32),
        scratch_shapes=[pltpu.VMEM((bm, bn), F32)],
        compiler_params=_params("parallel", "parallel", "arbitrary"),
        name="matmul_kgrid",
    )(a, b, residual)


def _gateup_body(h_ref, wg_ref, wu_ref, o_ref):
    h = h_ref[...]
    g = jnp.dot(h, wg_ref[...], preferred_element_type=F32)
    u = jnp.dot(h, wu_ref[...], preferred_element_type=F32)
    o_ref[...] = (g / (1.0 + jnp.exp(-g)) * u).astype(o_ref.dtype)


def gateup(h, wg, wu, bm_pref=1024, bn_pref=512):
    m, k = h.shape
    _, n = wg.shape
    bm, bn = _pick(m, bm_pref, 8), _pick(n, bn_pref)
    return pl.pallas_call(
        _gateup_body,
        grid=(m // bm, n // bn),
        in_specs=[pl.BlockSpec((bm, k), lambda i, j: (i, 0)),
                  pl.BlockSpec((k, bn), lambda i, j: (0, j)),
                  pl.BlockSpec((k, bn), lambda i, j: (0, j))],
        out_specs=pl.BlockSpec((bm, bn), lambda i, j: (i, j)),
        out_shape=jax.ShapeDtypeStruct((m, n), BF16),
        compiler_params=_params("parallel", "parallel"),
        name="gateup",
    )(h, wg, wu)


class Seqs:
    def __init__(self, b1, s1, b2, s2):
        assert b1 == 1 and s1 == 2 * s2, "layout assumes one prompt of twice the sample length"
        self.s1, self.s2, self.nb2 = s1, s2, b2
        self.p0 = b2 * s2
        self.t = self.p0 + s1

    def positions(self):
        return jnp.concatenate([jnp.arange(self.s2)] * self.nb2 + [jnp.arange(self.s1)])

    def bounds(self, r0, unit=1):
        p0, s1, s2 = self.p0 // unit, self.s1 // unit, self.s2 // unit
        is_p = r0 >= p0
        return jnp.where(is_p, p0, r0 // s2 * s2), jnp.where(is_p, s1, s2)


def _swap_quarters(y):
    lane = lax.broadcasted_iota(jnp.int32, y.shape, 1)
    first = (lane % (HEAD_DIM // 2)) < (HEAD_DIM // 4)
    return jnp.where(first, pltpu.roll(y, HEAD_DIM - HEAD_DIM // 4, 1), pltpu.roll(y, HEAD_DIM // 4, 1))


def _rope_body(x_ref, cos_ref, sin_ref, qg_ref, kg_ref, q_ref, k_ref, v_ref, *, nq, nk, scale):
    cos, sin = cos_ref[...], sin_ref[...]

    def norm_rope(x, gain):
        ms = jnp.mean(x * x, axis=-1, keepdims=True)
        y = x * lax.rsqrt(ms + RMS_EPS) * gain
        return y * cos + _swap_quarters(y) * sin

    for h in range(nq):
        sl = slice(h * HEAD_DIM, (h + 1) * HEAD_DIM)
        q_ref[:, sl] = (norm_rope(x_ref[:, sl], qg_ref[...]) * scale).astype(q_ref.dtype)
    ones = jnp.ones((x_ref.shape[0], HEAD_DIM), v_ref.dtype)
    for h in range(nk):
        src = slice((nq + h) * HEAD_DIM, (nq + h + 1) * HEAD_DIM)
        k_ref[:, h * HEAD_DIM:(h + 1) * HEAD_DIM] = norm_rope(x_ref[:, src], kg_ref[...]).astype(k_ref.dtype)
        vsrc = slice((nq + nk + h) * HEAD_DIM, (nq + nk + h + 1) * HEAD_DIM)
        v_ref[:, 2 * h * HEAD_DIM:(2 * h + 1) * HEAD_DIM] = x_ref[:, vsrc].astype(v_ref.dtype)
        v_ref[:, (2 * h + 1) * HEAD_DIM:(2 * h + 2) * HEAD_DIM] = ones


def rope_qk(qkv, cos, sin, q_gain, k_gain, nq, nk):
    t, w = qkv.shape
    bm = _pick(t, 256, 8)
    row = lambda i: (i, 0)
    fixed = lambda i: (0, 0)
    return pl.pallas_call(
        functools.partial(_rope_body, nq=nq, nk=nk, scale=LOG2E * HEAD_DIM ** -0.5),
        grid=(t // bm,),
        in_specs=[pl.BlockSpec((bm, w), row), pl.BlockSpec((bm, HEAD_DIM), row), pl.BlockSpec((bm, HEAD_DIM), row),
                  pl.BlockSpec((1, HEAD_DIM), fixed), pl.BlockSpec((1, HEAD_DIM), fixed)],
        out_specs=[pl.BlockSpec((bm, nq * HEAD_DIM), row), pl.BlockSpec((bm, nk * HEAD_DIM), row),
                   pl.BlockSpec((bm, 2 * nk * HEAD_DIM), row)],
        out_shape=[jax.ShapeDtypeStruct((t, nq * HEAD_DIM), BF16), jax.ShapeDtypeStruct((t, nk * HEAD_DIM), BF16),
                   jax.ShapeDtypeStruct((t, 2 * nk * HEAD_DIM), BF16)],
        compiler_params=_params("parallel"),
        name="rope_qk",
    )(qkv, cos, sin, q_gain.reshape(1, -1).astype(F32), k_gain.reshape(1, -1).astype(F32))


def rope_tables(seqs):
    half = HEAD_DIM // 2
    inv = ROPE_THETA ** (-jnp.arange(0, half, 2, dtype=F32) / half)
    t = seqs.positions()
    ang_r = (t // GRID_W).astype(F32)[:, None] * inv
    ang_c = (t % GRID_W).astype(F32)[:, None] * inv
    cr, sr, cc, sc = jnp.cos(ang_r), jnp.sin(ang_r), jnp.cos(ang_c), jnp.sin(ang_c)
    return jnp.concatenate([cr, cr, cc, cc], axis=-1), jnp.concatenate([-sr, sr, -sc, sc], axis=-1)


def _kv_window_spec(seqs, bq, width, col0):
    return pl.BlockSpec((pl.Element(seqs.s1), pl.Element(width)),
                        lambda h, i: (pl.multiple_of(seqs.bounds(i * bq)[0], bq), pl.multiple_of(col0(h), LANES)))


FLASH_ROW_CHUNK = 32


def _flash_pipeline(nblk, scores, probs, pv_scale, s_bufs, p_bufs):
    (s_e, s_o), (p_e, p_o) = s_bufs, p_bufs
    scores(0, s_e)
    scores(1, s_o)
    probs(0, s_e, p_e)

    def pair(jj, carry):
        j = 2 * jj + 1
        scores(j + 1, s_e)
        probs(j, s_o, p_o)
        pv_scale(j - 1, p_e, True)
        scores(j + 2, s_o)
        probs(j + 1, s_e, p_e)
        pv_scale(j, p_o, True)
        return carry

    lax.fori_loop(0, (nblk - 2) // 2, pair, 0)
    probs(nblk - 1, s_o, p_o)
    pv_scale(nblk - 2, p_e, True)
    pv_scale(nblk - 1, p_o, False)


def _flash_scratch(m_rows, bkv, acc_width, n_stats):
    return ([pltpu.VMEM((m_rows, bkv), F32)] * 2 + [pltpu.VMEM((m_rows, bkv), BF16)] * 2
            + [pltpu.VMEM((m_rows, acc_width), F32)] + [pltpu.VMEM((m_rows, 1), F32)] * n_stats)


def _gqa_body(q_ref, k_ref, v_ref, o_ref, s_e, s_o, p_e, p_o, acc_ref, m_ref, alpha_ref, *, rep, bq, bkv, seqs):
    i = pl.program_id(1)
    nblk = seqs.bounds(i * bq)[1] // bkv
    q = q_ref[...]
    qs = jnp.concatenate([q[:, r * HEAD_DIM:(r + 1) * HEAD_DIM] for r in range(rep)], axis=0)
    rows = lambda j: pl.ds(pl.multiple_of(j * bkv, bkv), bkv)

    def scores(j, s_ref):
        s_ref[...] = lax.dot_general(qs, k_ref[rows(j), :], NT_DIMS, preferred_element_type=F32)

    def probs(j, s_ref, p_ref):
        for r0 in range(0, rep * bq, FLASH_ROW_CHUNK):
            rs = slice(r0, r0 + FLASH_ROW_CHUNK)
            s = s_ref[rs, :]
            m = m_ref[rs, :]
            m_new = jnp.maximum(m, jnp.max(s, axis=-1, keepdims=True))
            p_ref[rs, :] = jnp.exp2(s - m_new).astype(BF16)
            alpha_ref[rs, :] = jnp.exp2(m - m_new)
            m_ref[rs, :] = m_new

    def pv_scale(j, p_ref, rescale):
        acc = acc_ref[...] + jnp.dot(p_ref[...], v_ref[rows(j), :], preferred_element_type=F32)
        acc_ref[...] = acc * alpha_ref[...] if rescale else acc

    acc_ref[...] = jnp.zeros_like(acc_ref)
    m_ref[...] = jnp.full_like(m_ref, MASK_VALUE)
    _flash_pipeline(nblk, scores, probs, pv_scale, (s_e, s_o), (p_e, p_o))
    acc = acc_ref[...]
    o = acc[:, :HEAD_DIM] / acc[:, HEAD_DIM:]
    for r in range(rep):
        o_ref[:, r * HEAD_DIM:(r + 1) * HEAD_DIM] = o[r * bq:(r + 1) * bq].astype(o_ref.dtype)


def gqa_attention(q, k, v1, seqs, bq_pref=256, bkv_pref=1024):
    t, wq = q.shape
    nk = k.shape[1] // HEAD_DIM
    rep = wq // HEAD_DIM // nk
    bq = _pick(seqs.s2, bq_pref, 16)
    bkv = _pick(seqs.s2 // 2, bkv_pref)
    return pl.pallas_call(
        functools.partial(_gqa_body, rep=rep, bq=bq, bkv=bkv, seqs=seqs),
        grid=(nk, t // bq),
        in_specs=[pl.BlockSpec((bq, rep * HEAD_DIM), lambda g, i: (i, g)),
                  _kv_window_spec(seqs, bq, HEAD_DIM, lambda g: g * HEAD_DIM),
                  _kv_window_spec(seqs, bq, 2 * HEAD_DIM, lambda g: g * 2 * HEAD_DIM)],
        out_specs=pl.BlockSpec((bq, rep * HEAD_DIM), lambda g, i: (i, g)),
        out_shape=jax.ShapeDtypeStruct((t, wq), BF16),
        scratch_shapes=_flash_scratch(rep * bq, bkv, 2 * HEAD_DIM, 2),
        compiler_params=_params("parallel", "arbitrary"),
        name="gqa_attention",
    )(q, k, v1)


def _diff_body(q_ref, k_ref, v_ref, slope_ref, lq1_ref, lk1_ref, lq2_ref, lk2_ref, g_ref, o_ref,
               s_e, s_o, p_e, p_o, acc_ref, m_ref, alpha_ref, l_ref, rel_ref, *, bq, bkv, seqs, lambda_init):
    i = pl.program_id(1)
    start, slen = seqs.bounds(i * bq)
    nblk = slen // bkv
    qpos0 = i * bq - start
    q = q_ref[...]
    q0, q1 = q[:, :HEAD_DIM], q[:, HEAD_DIM:]
    slope = slope_ref[0][:, :1] * LOG2E
    rel = (lax.broadcasted_iota(jnp.int32, (bq, bkv), 0) - lax.broadcasted_iota(jnp.int32, (bq, bkv), 1))
    rel_ref[...] = slope * rel.astype(F32)
    rows = lambda j: pl.ds(pl.multiple_of(j * bkv, bkv), bkv)

    def scores(j, s_ref):
        k = k_ref[rows(j), :]
        s_ref[:bq] = lax.dot_general(q0, k[:, :HEAD_DIM], NT_DIMS, preferred_element_type=F32)
        s_ref[bq:] = lax.dot_general(q1, k[:, HEAD_DIM:], NT_DIMS, preferred_element_type=F32)

    def probs(j, s_ref, p_ref):
        shift = slope * (qpos0 - j * bkv).astype(F32)
        for r0 in range(0, 2 * bq, FLASH_ROW_CHUNK):
            rs = slice(r0, r0 + FLASH_ROW_CHUNK)
            rq = slice(r0 % bq, r0 % bq + FLASH_ROW_CHUNK)
            s = s_ref[rs, :] - jnp.abs(rel_ref[rq, :] + shift)
            m = m_ref[rs, :]
            m_new = jnp.maximum(m, jnp.max(s, axis=-1, keepdims=True))
            alpha = jnp.exp2(m - m_new)
            p = jnp.exp2(s - m_new)
            p_ref[rs, :] = p.astype(BF16)
            l_ref[rs, :] = alpha * l_ref[rs, :] + sum(p[:, c:c + LANES] for c in range(0, bkv, LANES))
            alpha_ref[rs, :] = alpha
            m_ref[rs, :] = m_new

    def pv_scale(j, p_ref, rescale):
        acc = acc_ref[...] + jnp.dot(p_ref[...], v_ref[rows(j), :], preferred_element_type=F32)
        acc_ref[...] = acc * alpha_ref[...] if rescale else acc

    acc_ref[...] = jnp.zeros_like(acc_ref)
    m_ref[...] = jnp.full_like(m_ref, MASK_VALUE)
    l_ref[...] = jnp.zeros_like(l_ref)
    _flash_pipeline(nblk, scores, probs, pv_scale, (s_e, s_o), (p_e, p_o))
    o = acc_ref[...] / jnp.sum(l_ref[...], axis=-1, keepdims=True)
    lam = (jnp.exp(jnp.sum(lq1_ref[...] * lk1_ref[...], axis=-1, keepdims=True))
           - jnp.exp(jnp.sum(lq2_ref[...] * lk2_ref[...], axis=-1, keepdims=True)) + lambda_init)
    d = o[:bq] - lam * o[bq:]
    ms = jnp.mean(d * d, axis=-1, keepdims=True)
    o_ref[...] = (d * lax.rsqrt(ms + RMS_EPS) * g_ref[...] * (1.0 - lambda_init)).astype(o_ref.dtype)


def diff_attention(qkv, slopes, lq1, lk1, lq2, lk2, subln_gain, seqs, lambda_init, bq_pref=512, bkv_pref=1024):
    t, w = qkv.shape
    hw = 2 * HEAD_DIM
    nh = w // (3 * hw)
    bq = _pick(seqs.s2, bq_pref, 16)
    bkv = _pick(seqs.s2 // 2, bkv_pref)
    vec = lambda a: a.reshape(1, -1).astype(F32)
    vec_spec = lambda n: pl.BlockSpec((1, n), lambda h, i: (0, 0))
    return pl.pallas_call(
        functools.partial(_diff_body, bq=bq, bkv=bkv, seqs=seqs, lambda_init=lambda_init),
        grid=(nh, t // bq),
        in_specs=[pl.BlockSpec((bq, hw), lambda h, i: (i, h)),
                  _kv_window_spec(seqs, bq, hw, lambda h: (nh + h) * hw),
                  _kv_window_spec(seqs, bq, hw, lambda h: (2 * nh + h) * hw),
                  pl.BlockSpec((1, 1, LANES), lambda h, i: (h, 0, 0)),
                  vec_spec(HEAD_DIM), vec_spec(HEAD_DIM), vec_spec(HEAD_DIM), vec_spec(HEAD_DIM), vec_spec(hw)],
        out_specs=pl.BlockSpec((bq, hw), lambda h, i: (i, h)),
        out_shape=jax.ShapeDtypeStruct((t, nh * hw), BF16),
        scratch_shapes=_flash_scratch(2 * bq, bkv, hw, 2) + [pltpu.VMEM((2 * bq, LANES), F32), pltpu.VMEM((bq, bkv), F32)],
        compiler_params=_params("parallel", "arbitrary"),
        name="diff_attention",
    )(qkv, qkv, qkv, jnp.broadcast_to(slopes.astype(F32)[:, None, None], (nh, 1, LANES)),
      vec(lq1), vec(lk1), vec(lq2), vec(lk2), vec(subln_gain))


def _dilated_body(q_ref, kp_ref, kc_ref, kn_ref, vp_ref, vc_ref, vn_ref, o_ref, lse_ref, *, nh, dilation, radius, seqs):
    n = pl.program_id(1)
    qb = Q_BLOCK
    row0 = n * qb
    seq_start, seq_len = seqs.bounds(row0, dilation)
    kw = qb + 2 * radius
    r_i = lax.broadcasted_iota(jnp.int32, (qb, kw), 0)
    c_i = lax.broadcasted_iota(jnp.int32, (qb, kw), 1)
    jrel = c_i - radius - r_i
    kabs = row0 - radius + c_i
    valid = (jnp.abs(jrel) <= radius) & (kabs >= seq_start) & (kabs < seq_start + seq_len)
    dist = (dilation * jnp.abs(jrel)).astype(F32)
    lane = lax.broadcasted_iota(jnp.int32, (qb, LANES), 1)
    lse_tile = jnp.zeros((qb, LANES), F32)
    for h in range(nh):
        sl = slice(h * HEAD_DIM, (h + 1) * HEAD_DIM)
        slope = LOG2E * 2.0 ** (-8.0 * (h + 1) / nh)
        k = jnp.concatenate([kp_ref[qb - radius:, sl], kc_ref[:, sl], kn_ref[:radius, sl]], axis=0)
        v = jnp.concatenate([vp_ref[qb - radius:, sl], vc_ref[:, sl], vn_ref[:radius, sl]], axis=0)
        s = lax.dot_general(q_ref[:, sl], k, NT_DIMS, preferred_element_type=F32)
        s = jnp.where(valid, s - slope * dist, MASK_VALUE)
        m = jnp.max(s, axis=-1, keepdims=True)
        p = jnp.exp2(s - m)
        l = jnp.sum(p, axis=-1, keepdims=True)
        o_ref[:, sl] = jnp.dot(p.astype(BF16), v, preferred_element_type=F32) / l
        lse_tile = jnp.where(lane == h, m + jnp.log2(l), lse_tile)
    lse_ref[...] = lse_tile


def dilated_group(qkv, nh, window, dilation, seqs):
    t, w = qkv.shape
    hw = nh * HEAD_DIM
    radius = window // (2 * dilation)
    assert radius <= Q_BLOCK and seqs.s2 % (dilation * Q_BLOCK) == 0
    rows = t // dilation
    nblk = rows // Q_BLOCK
    view = qkv.reshape(rows, dilation * w)

    def spec(which, shift):
        return pl.BlockSpec((Q_BLOCK, hw), lambda c, n: (jnp.clip(n + shift, 0, nblk - 1), c * 3 + which))

    o, lse = pl.pallas_call(
        functools.partial(_dilated_body, nh=nh, dilation=dilation, radius=radius, seqs=seqs),
        grid=(dilation, nblk),
        in_specs=[spec(0, 0), spec(1, -1), spec(1, 0), spec(1, 1), spec(2, -1), spec(2, 0), spec(2, 1)],
        out_specs=[pl.BlockSpec((Q_BLOCK, hw), lambda c, n: (n, c)),
                   pl.BlockSpec((Q_BLOCK, LANES), lambda c, n: (n, c))],
        out_shape=[jax.ShapeDtypeStruct((rows, dilation * hw), F32),
                   jax.ShapeDtypeStruct((rows, dilation * LANES), F32)],
        compiler_params=_params("parallel", "parallel"),
        name=f"dilated_d{dilation}",
    )(view, view, view, view, view, view, view)
    return o.reshape(t, hw), lse.reshape(t, LANES)


def _merge_body(*refs, ng, nh):
    o_refs, lse_refs, out_ref = refs[:ng], refs[ng:2 * ng], refs[2 * ng]
    lses = [r[...] for r in lse_refs]
    mx = functools.reduce(jnp.maximum, lses)
    es = [jnp.exp2(x - mx) for x in lses]
    tot = functools.reduce(lambda a, b: a + b, es)
    ws = [e / tot for e in es]
    for h in range(nh):
        sl = slice(h * HEAD_DIM, (h + 1) * HEAD_DIM)
        acc = ws[0][:, h:h + 1] * o_refs[0][:, sl]
        for gi in range(1, ng):
            acc = acc + ws[gi][:, h:h + 1] * o_refs[gi][:, sl]
        out_ref[:, sl] = acc.astype(out_ref.dtype)


def merge_groups(outs, lses, nh):
    t, hw = outs[0].shape
    ng = len(outs)
    bm = _pick(t, 256, 8)
    return pl.pallas_call(
        functools.partial(_merge_body, ng=ng, nh=nh),
        grid=(t // bm,),
        in_specs=[pl.BlockSpec((bm, hw), lambda i: (i, 0))] * ng + [pl.BlockSpec((bm, LANES), lambda i: (i, 0))] * ng,
        out_specs=pl.BlockSpec((bm, hw), lambda i: (i, 0)),
        out_shape=jax.ShapeDtypeStruct((t, hw), BF16),
        compiler_params=_params("parallel"),
        name="dilated_merge",
    )(*outs, *lses)


def _na_block_maps(seqs):
    def maps(rb):
        r0 = rb * NA_BLOCK_ROWS
        start, nrows = seqs.bounds(r0, GRID_W)
        rl = r0 - start
        ws = start + jnp.clip(rl - NA_ROWS // 2, 0, nrows - NA_WIN_ROWS)
        variant = jnp.where(rl == 0, 0, jnp.where(rl == nrows - NA_BLOCK_ROWS, 2, 1))
        return ws, variant

    return maps


def _na_body(q_ref, k_ref, v_ref, b_ref, o_ref, *, nh):
    sls = [slice(h * HEAD_DIM, (h + 1) * HEAD_DIM) for h in range(nh)]
    ss = [lax.dot_general(q_ref[:, sl], k_ref[:, sl], NT_DIMS, preferred_element_type=F32) + b_ref[0, h]
          for h, sl in enumerate(sls)]
    for sl, s in zip(sls, ss):
        m = jnp.max(s, axis=-1, keepdims=True)
        p = jnp.exp2(s - m)
        l = jnp.sum(p, axis=-1, keepdims=True)
        o_ref[:, sl] = (jnp.dot(p.astype(BF16), v_ref[:, sl], preferred_element_type=F32) / l).astype(o_ref.dtype)


def na_bias_tables(rpb):
    c = jnp.arange(GRID_W)
    cs = jnp.clip(c - NA_COLS // 2, 0, GRID_W - NA_COLS)
    col_ok = (c[None, :] >= cs[:, None]) & (c[None, :] < cs[:, None] + NA_COLS)
    col_idx = jnp.clip(c[None, :] - c[:, None] + NA_COLS - 1, 0, 2 * NA_COLS - 2)
    rpb_c = jnp.where(col_ok[None, None], rpb.astype(F32)[:, :, col_idx] * LOG2E, MASK_VALUE)
    q = np.arange(NA_BLOCK_ROWS)
    half = NA_ROWS // 2
    variants = [(np.zeros_like(q), q - half),
                (q, np.zeros_like(q)),
                (np.full_like(q, NA_WIN_ROWS - NA_ROWS), q)]
    kr = np.arange(NA_WIN_ROWS)
    tabs = []
    for off, e in variants:
        rr = kr[None, :] - off[:, None]
        valid = (rr >= 0) & (rr < NA_ROWS)
        row_off = np.clip(rr + half - 1 - e[:, None], 0, 2 * NA_ROWS - 2)
        tab = jnp.where(jnp.asarray(valid)[None, :, :, None, None], rpb_c[:, row_off], MASK_VALUE)
        tabs.append(jnp.transpose(tab, (0, 1, 3, 2, 4)).reshape(rpb.shape[0], NA_BLOCK_ROWS * GRID_W,
                                                               NA_WIN_ROWS * GRID_W))
    return jnp.stack(tabs)


def neighbourhood_attention(qkv, bias, nh, seqs):
    t, w = qkv.shape
    hw = nh * HEAD_DIM
    hps = min(NA_HEADS_PER_STEP, nh)
    gw = hps * HEAD_DIM
    assert seqs.s2 % (NA_BLOCK_ROWS * GRID_W) == 0 and seqs.s2 >= NA_WIN_ROWS * GRID_W
    maps = _na_block_maps(seqs)
    bq = NA_BLOCK_ROWS * GRID_W
    kw = NA_WIN_ROWS * GRID_W

    def win_spec(col0):
        return pl.BlockSpec((pl.Element(kw), pl.Element(gw)),
                            lambda g, rb: (pl.multiple_of(maps(rb)[0] * GRID_W, GRID_W), pl.multiple_of(col0 + g * gw, LANES)))

    return pl.pallas_call(
        functools.partial(_na_body, nh=hps),
        grid=(nh // hps, t // bq),
        in_specs=[pl.BlockSpec((bq, gw), lambda g, rb: (rb, g)),
                  win_spec(hw), win_spec(2 * hw),
                  pl.BlockSpec((1, hps, bq, kw), lambda g, rb: (maps(rb)[1], g, 0, 0))],
        out_specs=pl.BlockSpec((bq, gw), lambda g, rb: (rb, g)),
        out_shape=jax.ShapeDtypeStruct((t, hw), BF16),
        compiler_params=_params("parallel", "arbitrary"),
        name="neighbourhood_attention",
    )(qkv, qkv, qkv, bias)


def _scale_q_cols(w, widths, scale):
    parts, c0 = [], 0
    for is_q, wd in widths:
        blk = w[:, c0:c0 + wd]
        parts.append(blk * scale if is_q else blk)
        c0 += wd
    return jnp.concatenate(parts, axis=1)


def kernel(x_prompt, x_sample, norm_mix, norm_ffn, norm_final, a_w_qkv, a_q_gain, a_k_gain, a_w_o, b_w_qkv,
           b_lambda_q1, b_lambda_k1, b_lambda_q2, b_lambda_k2, b_subln_gain, b_w_o, c_w_qkv, c_w_o, d_w_qkv, d_rpb,
           d_w_o, ffn_w_gate, ffn_w_up, ffn_w_down):
    b1, s1, d = x_prompt.shape
    b2, s2, _ = x_sample.shape
    seqs = Seqs(b1, s1, b2, s2)
    depth = norm_mix.shape[0]
    x = jnp.concatenate([x_sample.reshape(b2 * s2, d), x_prompt.reshape(b1 * s1, d)], axis=0)

    a_nk = (a_w_qkv.shape[1] - d) // (2 * HEAD_DIM)
    a_nq = d // HEAD_DIM
    b_nh = b_w_o.shape[0] // (2 * HEAD_DIM)
    c_nh = c_w_o.shape[0] // HEAD_DIM
    d_nh = d_w_o.shape[0] // HEAD_DIM
    ng = len(C_CONFIGS)

    qscale = LOG2E * HEAD_DIM ** -0.5
    wa = a_w_qkv.astype(BF16)
    wb = _scale_q_cols(b_w_qkv, [(True, b_nh * 2 * HEAD_DIM), (False, 2 * b_nh * 2 * HEAD_DIM)], qscale).astype(BF16)
    chw = c_nh * HEAD_DIM
    wc = _scale_q_cols(c_w_qkv, [(True, chw), (False, 2 * chw)] * ng, qscale).astype(BF16)
    wcs = [wc[:, g * 3 * chw:(g + 1) * 3 * chw] for g in range(ng)]
    wd = _scale_q_cols(d_w_qkv, [(True, d_nh * HEAD_DIM), (False, 2 * d_nh * HEAD_DIM)], qscale).astype(BF16)
    wo = {0: a_w_o.astype(BF16), 1: b_w_o.astype(BF16), 2: c_w_o.astype(BF16), 3: d_w_o.astype(BF16)}
    dff = ffn_w_gate.shape[2]
    pad = (-dff) % FF_ALIGN
    w_gate = jnp.pad(ffn_w_gate.astype(BF16), ((0, 0), (0, 0), (0, pad)))
    w_up = jnp.pad(ffn_w_up.astype(BF16), ((0, 0), (0, 0), (0, pad)))
    w_down = jnp.pad(ffn_w_down.astype(BF16), ((0, 0), (0, pad), (0, 0)))

    cos, sin = rope_tables(seqs)
    slopes_b = 2.0 ** (-8.0 * jnp.arange(1, b_nh + 1, dtype=F32) / b_nh)
    na_bias = na_bias_tables(d_rpb)

    for i in range(depth):
        h = rmsnorm(x, norm_mix[i], BF16)
        kind = i % 4
        if kind == 0:
            qkv = matmul(h, wa, F32)
            q, k, v1 = rope_qk(qkv, cos, sin, a_q_gain, a_k_gain, a_nq, a_nk)
            o = gqa_attention(q, k, v1, seqs)
        elif kind == 1:
            qkv = matmul(h, wb, BF16)
            lambda_init = 0.8 - 0.6 * math.exp(-0.3 * i)
            o = diff_attention(qkv, slopes_b, b_lambda_q1, b_lambda_k1, b_lambda_q2, b_lambda_k2, b_subln_gain, seqs,
                               lambda_init)
        elif kind == 2:
            outs, lses = zip(*[dilated_group(matmul(h, wcs[g], BF16), c_nh, win, dil, seqs)
                               for g, (win, dil) in enumerate(C_CONFIGS)])
            o = merge_groups(outs, lses, c_nh)
        else:
            qkv = matmul(h, wd, BF16)
            o = neighbourhood_attention(qkv, na_bias, d_nh, seqs)
        x = matmul(o, wo[kind], F32, residual=x)
        h = rmsnorm(x, norm_ffn[i], BF16)
        a = gateup(h, w_gate[i], w_up[i])
        x = matmul_kgrid_res(a, w_down[i], x)

    y_sample = rmsnorm(x, norm_final, F32, row0=0, rows=b2 * s2).reshape(b2, s2, d)
    y_prompt = rmsnorm(x, norm_final, F32, row0=b2 * s2, rows=b1 * s1).reshape(b1, s1, d)
    return y_prompt, y_sample
```

```python
import functools
import math

import jax
import jax.numpy as jnp
import numpy as np
from jax import lax
from jax.experimental import pallas as pl
from jax.experimental.pallas import tpu as pltpu

HEAD_DIM = 128
GRID_W = 64
Q_BLOCK = 128
RMS_EPS = 1e-6
MASK_VALUE = -1e30
ROPE_THETA = 10000.0
C_CONFIGS = ((128, 1), (512, 4), (2048, 16))
NA_ROWS = 8
NA_COLS = 16
NA_BLOCK_ROWS = 4
NA_WIN_ROWS = NA_BLOCK_ROWS + NA_ROWS
NA_HEADS_PER_STEP = 8
LOG2E = math.log2(math.e)
LANES = 128
V7X_VMEM_LIMIT_BYTES = 56 * 1024 * 1024
FF_ALIGN = 1024

F32 = jnp.float32
BF16 = jnp.bfloat16
NT_DIMS = (((1,), (1,)), ((), ()))


def _params(*sem):
    return pltpu.CompilerParams(dimension_semantics=sem, vmem_limit_bytes=V7X_VMEM_LIMIT_BYTES)


def _pick(n, pref, align=LANES):
    if n <= pref:
        return n
    b = (pref // align) * align
    while b >= align:
        if n % b == 0:
            return b
        b -= align
    raise ValueError(f"no block for {n} under {pref}")


def _rmsnorm_body(x_ref, g_ref, o_ref):
    x = x_ref[...]
    ms = jnp.mean(x * x, axis=-1, keepdims=True)
    o_ref[...] = (x * lax.rsqrt(ms + RMS_EPS) * g_ref[...]).astype(o_ref.dtype)


def rmsnorm(x, gain, out_dtype, row0=0, rows=None):
    t, d = x.shape
    rows = t if rows is None else rows
    bm = _pick(math.gcd(rows, row0) if row0 else rows, 256, 8)
    off = row0 // bm
    return pl.pallas_call(
        _rmsnorm_body,
        grid=(rows // bm,),
        in_specs=[pl.BlockSpec((bm, d), lambda i: (i + off, 0)),
                  pl.BlockSpec((1, d), lambda i: (0, 0))],
        out_specs=pl.BlockSpec((bm, d), lambda i: (i, 0)),
        out_shape=jax.ShapeDtypeStruct((rows, d), out_dtype),
        compiler_params=_params("parallel"),
        name="rmsnorm",
    )(x, gain.reshape(1, d).astype(F32))


def _row_scale(acc, rstd):
    return jnp.concatenate([acc[:, c:c + LANES] * rstd for c in range(0, acc.shape[1], LANES)], axis=1)


def _lane_partial_sumsq(x):
    return sum(x[:, c:c + LANES] * x[:, c:c + LANES] for c in range(0, x.shape[1], LANES))


def _finish_rstd(ssq, d_model):
    total = jnp.sum(ssq, axis=-1, keepdims=True)
    return jnp.broadcast_to(lax.rsqrt(total / d_model + RMS_EPS), ssq.shape)


def _prep_body(x_ref, xb_ref, rstd_ref):
    x = x_ref[...]
    xb_ref[...] = x.astype(xb_ref.dtype)
    rstd_ref[...] = _finish_rstd(_lane_partial_sumsq(x), x.shape[1])


def prep_norm(x):
    t, d = x.shape
    bm = _pick(t, 256, 8)
    return pl.pallas_call(
        _prep_body,
        grid=(t // bm,),
        in_specs=[pl.BlockSpec((bm, d), lambda i: (i, 0))],
        out_specs=[pl.BlockSpec((bm, d), lambda i: (i, 0)), pl.BlockSpec((bm, LANES), lambda i: (i, 0))],
        out_shape=[jax.ShapeDtypeStruct((t, d), BF16), jax.ShapeDtypeStruct((t, LANES), F32)],
        compiler_params=_params("parallel"),
        name="prep_norm",
    )(x)


def _mm_scaled_body(a_ref, b_ref, s_ref, o_ref):
    acc = jnp.dot(a_ref[...], b_ref[...], preferred_element_type=F32)
    o_ref[...] = _row_scale(acc, s_ref[...]).astype(o_ref.dtype)


def matmul_scaled(a, b, rstd, out_dtype, bm_pref=1024, bn_pref=512):
    m, k = a.shape
    _, n = b.shape
    bm, bn = _pick(m, bm_pref, 8), _pick(n, bn_pref)
    return pl.pallas_call(
        _mm_scaled_body,
        grid=(m // bm, n // bn),
        in_specs=[pl.BlockSpec((bm, k), lambda i, j: (i, 0)),
                  pl.BlockSpec((k, bn), lambda i, j: (0, j)),
                  pl.BlockSpec((bm, LANES), lambda i, j: (i, 0))],
        out_specs=pl.BlockSpec((bm, bn), lambda i, j: (i, j)),
        out_shape=jax.ShapeDtypeStruct((m, n), out_dtype),
        compiler_params=_params("parallel", "parallel"),
        name="matmul",
    )(a, b, rstd)


def _emit_stream(x, first, last, d_model, o_ref, xb_ref, rstd_ref):
    o_ref[...] = x
    xb_ref[...] = x.astype(xb_ref.dtype)
    part = _lane_partial_sumsq(x)

    @pl.when(first)
    def _():
        rstd_ref[...] = part

    @pl.when(jnp.logical_not(first))
    def _():
        rstd_ref[...] += part

    @pl.when(last)
    def _():
        rstd_ref[...] = _finish_rstd(rstd_ref[...], d_model)


def _stream_out(m, n, bm, bn, idx):
    specs = [pl.BlockSpec((bm, bn), idx), pl.BlockSpec((bm, bn), idx),
             pl.BlockSpec((bm, LANES), lambda i, *_: (i, 0))]
    shapes = [jax.ShapeDtypeStruct((m, n), F32), jax.ShapeDtypeStruct((m, n), BF16),
              jax.ShapeDtypeStruct((m, LANES), F32)]
    return specs, shapes


def _mm_res_body(a_ref, b_ref, r_ref, o_ref, xb_ref, rstd_ref, *, d_model):
    j = pl.program_id(1)
    x = jnp.dot(a_ref[...], b_ref[...], preferred_element_type=F32) + r_ref[...]
    _emit_stream(x, j == 0, j == pl.num_programs(1) - 1, d_model, o_ref, xb_ref, rstd_ref)


def matmul_res(a, b, residual, bm_pref=1024, bn_pref=512):
    m, k = a.shape
    _, n = b.shape
    bm, bn = _pick(m, bm_pref, 8), _pick(n, bn_pref)
    out_specs, out_shape = _stream_out(m, n, bm, bn, lambda i, j: (i, j))
    return pl.pallas_call(
        functools.partial(_mm_res_body, d_model=n),
        grid=(m // bm, n // bn),
        in_specs=[pl.BlockSpec((bm, k), lambda i, j: (i, 0)),
                  pl.BlockSpec((k, bn), lambda i, j: (0, j)),
                  pl.BlockSpec((bm, bn), lambda i, j: (i, j))],
        out_specs=out_specs,
        out_shape=out_shape,
        compiler_params=_params("parallel", "arbitrary"),
        name="matmul_res",
    )(a, b, residual)


def _mm_kgrid_res_body(a_ref, b_ref, r_ref, o_ref, xb_ref, rstd_ref, acc_ref, *, d_model):
    j, kk = pl.program_id(1), pl.program_id(2)

    @pl.when(kk == 0)
    def _():
        acc_ref[...] = r_ref[...]

    acc_ref[...] += jnp.dot(a_ref[...], b_ref[...], preferred_element_type=F32)

    @pl.when(kk == pl.num_programs(2) - 1)
    def _():
        _emit_stream(acc_ref[...], j == 0, j == pl.num_programs(1) - 1, d_model, o_ref, xb_ref, rstd_ref)


def matmul_kgrid_res(a, b, residual, bm_pref=1024, bn_pref=1024, bk_pref=2816):
    m, k = a.shape
    _, n = b.shape
    bm, bn, bk = _pick(m, bm_pref, 8), _pick(n, bn_pref), _pick(k, bk_pref)
    out_specs, out_shape = _stream_out(m, n, bm, bn, lambda i, j, kk: (i, j))
    return pl.pallas_call(
        functools.partial(_mm_kgrid_res_body, d_model=n),
        grid=(m // bm, n // bn, k // bk),
        in_specs=[pl.BlockSpec((bm, bk), lambda i, j, kk: (i, kk)),
                  pl.BlockSpec((bk, bn), lambda i, j, kk: (kk, j)),
                  pl.BlockSpec((bm, bn), lambda i, j, kk: (i, j))],
        out_specs=out_specs,
        out_shape=out_shape,
        scratch_shapes=[pltpu.VMEM((bm, bn), F32)],
        compiler_params=_params("parallel", "arbitrary", "arbitrary"),
        name="matmul_kgrid",
    )(a, b, residual)


def _gateup_body(h_ref, wg_ref, wu_ref, s_ref, o_ref):
    h = h_ref[...]
    rstd = s_ref[...]
    g = _row_scale(jnp.dot(h, wg_ref[...], preferred_element_type=F32), rstd)
    u = _row_scale(jnp.dot(h, wu_ref[...], preferred_element_type=F32), rstd)
    o_ref[...] = (g / (1.0 + jnp.exp(-g)) * u).astype(o_ref.dtype)


def gateup(h, wg, wu, rstd, bm_pref=1024, bn_pref=512):
    m, k = h.shape
    _, n = wg.shape
    bm, bn = _pick(m, bm_pref, 8), _pick(n, bn_pref)
    return pl.pallas_call(
        _gateup_body,
        grid=(m // bm, n // bn),
        in_specs=[pl.BlockSpec((bm, k), lambda i, j: (i, 0)),
                  pl.BlockSpec((k, bn), lambda i, j: (0, j)),
                  pl.BlockSpec((k, bn), lambda i, j: (0, j)),
                  pl.BlockSpec((bm, LANES), lambda i, j: (i, 0))],
        out_specs=pl.BlockSpec((bm, bn), lambda i, j: (i, j)),
        out_shape=jax.ShapeDtypeStruct((m, n), BF16),
        compiler_params=_params("parallel", "parallel"),
        name="gateup",
    )(h, wg, wu, rstd)


class Seqs:
    def __init__(self, b1, s1, b2, s2):
        assert b1 == 1 and s1 == 2 * s2, "layout assumes one prompt of twice the sample length"
        self.s1, self.s2, self.nb2 = s1, s2, b2
        self.p0 = b2 * s2
        self.t = self.p0 + s1

    def positions(self):
        return jnp.concatenate([jnp.arange(self.s2)] * self.nb2 + [jnp.arange(self.s1)])

    def bounds(self, r0, unit=1):
        p0, s1, s2 = self.p0 // unit, self.s1 // unit, self.s2 // unit
        is_p = r0 >= p0
        return jnp.where(is_p, p0, r0 // s2 * s2), jnp.where(is_p, s1, s2)


def _swap_quarters(y):
    lane = lax.broadcasted_iota(jnp.int32, y.shape, 1)
    first = (lane % (HEAD_DIM // 2)) < (HEAD_DIM // 4)
    return jnp.where(first, pltpu.roll(y, HEAD_DIM - HEAD_DIM // 4, 1), pltpu.roll(y, HEAD_DIM // 4, 1))


def _rope_body(x_ref, cos_ref, sin_ref, qg_ref, kg_ref, q_ref, k_ref, v_ref, *, nq, nk, scale):
    cos, sin = cos_ref[...], sin_ref[...]

    def norm_rope(x, gain):
        ms = jnp.mean(x * x, axis=-1, keepdims=True)
        y = x * lax.rsqrt(ms + RMS_EPS) * gain
        return y * cos + _swap_quarters(y) * sin

    for h in range(nq):
        sl = slice(h * HEAD_DIM, (h + 1) * HEAD_DIM)
        q_ref[:, sl] = (norm_rope(x_ref[:, sl], qg_ref[...]) * scale).astype(q_ref.dtype)
    ones = jnp.ones((x_ref.shape[0], HEAD_DIM), v_ref.dtype)
    for h in range(nk):
        src = slice((nq + h) * HEAD_DIM, (nq + h + 1) * HEAD_DIM)
        k_ref[:, h * HEAD_DIM:(h + 1) * HEAD_DIM] = norm_rope(x_ref[:, src], kg_ref[...]).astype(k_ref.dtype)
        vsrc = slice((nq + nk + h) * HEAD_DIM, (nq + nk + h + 1) * HEAD_DIM)
        v_ref[:, 2 * h * HEAD_DIM:(2 * h + 1) * HEAD_DIM] = x_ref[:, vsrc].astype(v_ref.dtype)
        v_ref[:, (2 * h + 1) * HEAD_DIM:(2 * h + 2) * HEAD_DIM] = ones


def rope_qk(qkv, cos, sin, q_gain, k_gain, nq, nk):
    t, w = qkv.shape
    bm = _pick(t, 256, 8)
    row = lambda i: (i, 0)
    fixed = lambda i: (0, 0)
    return pl.pallas_call(
        functools.partial(_rope_body, nq=nq, nk=nk, scale=LOG2E * HEAD_DIM ** -0.5),
        grid=(t // bm,),
        in_specs=[pl.BlockSpec((bm, w), row), pl.BlockSpec((bm, HEAD_DIM), row), pl.BlockSpec((bm, HEAD_DIM), row),
                  pl.BlockSpec((1, HEAD_DIM), fixed), pl.BlockSpec((1, HEAD_DIM), fixed)],
        out_specs=[pl.BlockSpec((bm, nq * HEAD_DIM), row), pl.BlockSpec((bm, nk * HEAD_DIM), row),
                   pl.BlockSpec((bm, 2 * nk * HEAD_DIM), row)],
        out_shape=[jax.ShapeDtypeStruct((t, nq * HEAD_DIM), BF16), jax.ShapeDtypeStruct((t, nk * HEAD_DIM), BF16),
                   jax.ShapeDtypeStruct((t, 2 * nk * HEAD_DIM), BF16)],
        compiler_params=_params("parallel"),
        name="rope_qk",
    )(qkv, cos, sin, q_gain.reshape(1, -1).astype(F32), k_gain.reshape(1, -1).astype(F32))


def rope_tables(seqs):
    half = HEAD_DIM // 2
    inv = ROPE_THETA ** (-jnp.arange(0, half, 2, dtype=F32) / half)
    t = seqs.positions()
    ang_r = (t // GRID_W).astype(F32)[:, None] * inv
    ang_c = (t % GRID_W).astype(F32)[:, None] * inv
    cr, sr, cc, sc = jnp.cos(ang_r), jnp.sin(ang_r), jnp.cos(ang_c), jnp.sin(ang_c)
    return jnp.concatenate([cr, cr, cc, cc], axis=-1), jnp.concatenate([-sr, sr, -sc, sc], axis=-1)


def _kv_window_spec(seqs, bq, width, col0):
    return pl.BlockSpec((pl.Element(seqs.s1), pl.Element(width)),
                        lambda h, i: (pl.multiple_of(seqs.bounds(i * bq)[0], bq), pl.multiple_of(col0(h), LANES)))


FLASH_ROW_CHUNK = 32


def _flash_pipeline(nblk, scores, probs, pv_scale, s_bufs, p_bufs):
    (s_e, s_o), (p_e, p_o) = s_bufs, p_bufs
    scores(0, s_e)
    scores(1, s_o)
    probs(0, s_e, p_e)

    def pair(jj, carry):
        j = 2 * jj + 1
        scores(j + 1, s_e)
        probs(j, s_o, p_o)
        pv_scale(j - 1, p_e, True)
        scores(j + 2, s_o)
        probs(j + 1, s_e, p_e)
        pv_scale(j, p_o, True)
        return carry

    lax.fori_loop(0, (nblk - 2) // 2, pair, 0)
    probs(nblk - 1, s_o, p_o)
    pv_scale(nblk - 2, p_e, True)
    pv_scale(nblk - 1, p_o, False)


def _flash_scratch(m_rows, bkv, acc_width, n_stats):
    return ([pltpu.VMEM((m_rows, bkv), F32)] * 2 + [pltpu.VMEM((m_rows, bkv), BF16)] * 2
            + [pltpu.VMEM((m_rows, acc_width), F32)] + [pltpu.VMEM((m_rows, 1), F32)] * n_stats)


def _gqa_body(q_ref, k_ref, v_ref, o_ref, s_e, s_o, p_e, p_o, acc_ref, m_ref, alpha_ref, *, rep, bq, bkv, seqs):
    i = pl.program_id(1)
    nblk = seqs.bounds(i * bq)[1] // bkv
    q = q_ref[...]
    qs = jnp.concatenate([q[:, r * HEAD_DIM:(r + 1) * HEAD_DIM] for r in range(rep)], axis=0)
    rows = lambda j: pl.ds(pl.multiple_of(j * bkv, bkv), bkv)

    def scores(j, s_ref):
        s_ref[...] = lax.dot_general(qs, k_ref[rows(j), :], NT_DIMS, preferred_element_type=F32)

    def probs(j, s_ref, p_ref):
        for r0 in range(0, rep * bq, FLASH_ROW_CHUNK):
            rs = slice(r0, r0 + FLASH_ROW_CHUNK)
            s = s_ref[rs, :]
            m = m_ref[rs, :]
            m_new = jnp.maximum(m, jnp.max(s, axis=-1, keepdims=True))
            p_ref[rs, :] = jnp.exp2(s - m_new).astype(BF16)
            alpha_ref[rs, :] = jnp.exp2(m - m_new)
            m_ref[rs, :] = m_new

    def pv_scale(j, p_ref, rescale):
        acc = acc_ref[...] + jnp.dot(p_ref[...], v_ref[rows(j), :], preferred_element_type=F32)
        acc_ref[...] = acc * alpha_ref[...] if rescale else acc

    acc_ref[...] = jnp.zeros_like(acc_ref)
    m_ref[...] = jnp.full_like(m_ref, MASK_VALUE)
    _flash_pipeline(nblk, scores, probs, pv_scale, (s_e, s_o), (p_e, p_o))
    acc = acc_ref[...]
    o = acc[:, :HEAD_DIM] / acc[:, HEAD_DIM:]
    for r in range(rep):
        o_ref[:, r * HEAD_DIM:(r + 1) * HEAD_DIM] = o[r * bq:(r + 1) * bq].astype(o_ref.dtype)


def gqa_attention(q, k, v1, seqs, bq_pref=256, bkv_pref=1024):
    t, wq = q.shape
    nk = k.shape[1] // HEAD_DIM
    rep = wq // HEAD_DIM // nk
    bq = _pick(seqs.s2, bq_pref, 16)
    bkv = _pick(seqs.s2 // 2, bkv_pref)
    return pl.pallas_call(
        functools.partial(_gqa_body, rep=rep, bq=bq, bkv=bkv, seqs=seqs),
        grid=(nk, t // bq),
        in_specs=[pl.BlockSpec((bq, rep * HEAD_DIM), lambda g, i: (i, g)),
                  _kv_window_spec(seqs, bq, HEAD_DIM, lambda g: g * HEAD_DIM),
                  _kv_window_spec(seqs, bq, 2 * HEAD_DIM, lambda g: g * 2 * HEAD_DIM)],
        out_specs=pl.BlockSpec((bq, rep * HEAD_DIM), lambda g, i: (i, g)),
        out_shape=jax.ShapeDtypeStruct((t, wq), BF16),
        scratch_shapes=_flash_scratch(rep * bq, bkv, 2 * HEAD_DIM, 2),
        compiler_params=_params("parallel", "arbitrary"),
        name="gqa_attention",
    )(q, k, v1)


def _diff_body(q_ref, k_ref, v_ref, slope_ref, tab_ref, lq1_ref, lk1_ref, lq2_ref, lk2_ref, g_ref, o_ref,
               s_e, s_o, p_e, p_o, acc_ref, m_ref, alpha_ref, l_ref, *, bq, bkv, seqs, lambda_init):
    i = pl.program_id(1)
    start, slen = seqs.bounds(i * bq)
    nblk = slen // bkv
    qpos0 = i * bq - start
    q = q_ref[...]
    q0, q1 = q[:, :HEAD_DIM], q[:, HEAD_DIM:]
    slope = slope_ref[0][:, :1] * LOG2E
    rows = lambda j: pl.ds(pl.multiple_of(j * bkv, bkv), bkv)

    def scores(j, s_ref):
        k = k_ref[rows(j), :]
        s_ref[:bq] = lax.dot_general(q0, k[:, :HEAD_DIM], NT_DIMS, preferred_element_type=F32)
        s_ref[bq:] = lax.dot_general(q1, k[:, HEAD_DIM:], NT_DIMS, preferred_element_type=F32)

    def probs(j, s_ref, p_ref):
        subs = []
        for c0 in range(0, bkv, bq):
            lead = qpos0 - (j * bkv + c0)
            kind = jnp.where(lead == 0, 2, jnp.where(lead > 0, 0, 1))
            subs.append((c0, kind, slope * jnp.abs(lead).astype(F32)))
        for r0 in range(0, 2 * bq, FLASH_ROW_CHUNK):
            rs = slice(r0, r0 + FLASH_ROW_CHUNK)
            rq = slice(r0 % bq, r0 % bq + FLASH_ROW_CHUNK)
            ss = [s_ref[rs, c0:c0 + bq] - tab_ref[0, kind, rq, :] for c0, kind, _ in subs]
            mx = functools.reduce(jnp.maximum, [jnp.max(s, axis=-1, keepdims=True) - shift
                                                for s, (_, _, shift) in zip(ss, subs)])
            m = m_ref[rs, :]
            m_new = jnp.maximum(m, mx)
            alpha = jnp.exp2(m - m_new)
            lsum = alpha * l_ref[rs, :]
            for s, (c0, _, shift) in zip(ss, subs):
                p = jnp.exp2(s - (m_new + shift))
                p_ref[rs, c0:c0 + bq] = p.astype(BF16)
                lsum = lsum + sum(p[:, c:c + LANES] for c in range(0, bq, LANES))
            l_ref[rs, :] = lsum
            alpha_ref[rs, :] = alpha
            m_ref[rs, :] = m_new

    def pv_scale(j, p_ref, rescale):
        acc = acc_ref[...] + jnp.dot(p_ref[...], v_ref[rows(j), :], preferred_element_type=F32)
        acc_ref[...] = acc * alpha_ref[...] if rescale else acc

    acc_ref[...] = jnp.zeros_like(acc_ref)
    m_ref[...] = jnp.full_like(m_ref, MASK_VALUE)
    l_ref[...] = jnp.zeros_like(l_ref)
    _flash_pipeline(nblk, scores, probs, pv_scale, (s_e, s_o), (p_e, p_o))
    o = acc_ref[...] / jnp.sum(l_ref[...], axis=-1, keepdims=True)
    lam = (jnp.exp(jnp.sum(lq1_ref[...] * lk1_ref[...], axis=-1, keepdims=True))
           - jnp.exp(jnp.sum(lq2_ref[...] * lk2_ref[...], axis=-1, keepdims=True)) + lambda_init)
    d = o[:bq] - lam * o[bq:]
    ms = jnp.mean(d * d, axis=-1, keepdims=True)
    o_ref[...] = (d * lax.rsqrt(ms + RMS_EPS) * g_ref[...] * (1.0 - lambda_init)).astype(o_ref.dtype)


def diff_attention(qkv, slopes, lq1, lk1, lq2, lk2, subln_gain, seqs, lambda_init, bq_pref=512, bkv_pref=1024):
    t, w = qkv.shape
    hw = 2 * HEAD_DIM
    nh = w // (3 * hw)
    bq = _pick(seqs.s2 // 2, bq_pref)
    bkv = _pick(seqs.s2 // 2, bkv_pref)
    assert bkv % bq == 0
    rel = (jnp.arange(bq)[:, None] - jnp.arange(bq)[None, :]).astype(F32) * (LOG2E * slopes.astype(F32))[:, None, None]
    tabs = jnp.stack([rel, -rel, jnp.abs(rel)], axis=1)
    vec = lambda a: a.reshape(1, -1).astype(F32)
    vec_spec = lambda n: pl.BlockSpec((1, n), lambda h, i: (0, 0))
    return pl.pallas_call(
        functools.partial(_diff_body, bq=bq, bkv=bkv, seqs=seqs, lambda_init=lambda_init),
        grid=(nh, t // bq),
        in_specs=[pl.BlockSpec((bq, hw), lambda h, i: (i, h)),
                  _kv_window_spec(seqs, bq, hw, lambda h: (nh + h) * hw),
                  _kv_window_spec(seqs, bq, hw, lambda h: (2 * nh + h) * hw),
                  pl.BlockSpec((1, 1, LANES), lambda h, i: (h, 0, 0)),
                  pl.BlockSpec((1, 3, bq, bq), lambda h, i: (h, 0, 0, 0)),
                  vec_spec(HEAD_DIM), vec_spec(HEAD_DIM), vec_spec(HEAD_DIM), vec_spec(HEAD_DIM), vec_spec(hw)],
        out_specs=pl.BlockSpec((bq, hw), lambda h, i: (i, h)),
        out_shape=jax.ShapeDtypeStruct((t, nh * hw), BF16),
        scratch_shapes=_flash_scratch(2 * bq, bkv, hw, 2) + [pltpu.VMEM((2 * bq, LANES), F32)],
        compiler_params=_params("parallel", "arbitrary"),
        name="diff_attention",
    )(qkv, qkv, qkv, jnp.broadcast_to(slopes.astype(F32)[:, None, None], (nh, 1, LANES)), tabs,
      vec(lq1), vec(lk1), vec(lq2), vec(lk2), vec(subln_gain))


def _dilated_body(q_ref, kp_ref, kc_ref, kn_ref, vp_ref, vc_ref, vn_ref, o_ref, lse_ref, *, nh, dilation, radius, seqs):
    n = pl.program_id(1)
    qb = Q_BLOCK
    row0 = n * qb
    seq_start, seq_len = seqs.bounds(row0, dilation)
    kw = qb + 2 * radius
    r_i = lax.broadcasted_iota(jnp.int32, (qb, kw), 0)
    c_i = lax.broadcasted_iota(jnp.int32, (qb, kw), 1)
    jrel = c_i - radius - r_i
    kabs = row0 - radius + c_i
    valid = (jnp.abs(jrel) <= radius) & (kabs >= seq_start) & (kabs < seq_start + seq_len)
    dist = (dilation * jnp.abs(jrel)).astype(F32)
    lane = lax.broadcasted_iota(jnp.int32, (qb, LANES), 1)
    lse_tile = jnp.zeros((qb, LANES), F32)
    for h in range(nh):
        sl = slice(h * HEAD_DIM, (h + 1) * HEAD_DIM)
        slope = LOG2E * 2.0 ** (-8.0 * (h + 1) / nh)
        k = jnp.concatenate([kp_ref[qb - radius:, sl], kc_ref[:, sl], kn_ref[:radius, sl]], axis=0)
        v = jnp.concatenate([vp_ref[qb - radius:, sl], vc_ref[:, sl], vn_ref[:radius, sl]], axis=0)
        s = lax.dot_general(q_ref[:, sl], k, NT_DIMS, preferred_element_type=F32)
        s = jnp.where(valid, s - slope * dist, MASK_VALUE)
        m = jnp.max(s, axis=-1, keepdims=True)
        p = jnp.exp2(s - m)
        l = jnp.sum(p, axis=-1, keepdims=True)
        o_ref[:, sl] = jnp.dot(p.astype(BF16), v, preferred_element_type=F32) / l
        lse_tile = jnp.where(lane == h, m + jnp.log2(l), lse_tile)
    lse_ref[...] = lse_tile


def dilated_group(qkv, nh, window, dilation, seqs):
    t, w = qkv.shape
    hw = nh * HEAD_DIM
    radius = window // (2 * dilation)
    assert radius <= Q_BLOCK and seqs.s2 % (dilation * Q_BLOCK) == 0
    rows = t // dilation
    nblk = rows // Q_BLOCK
    view = qkv.reshape(rows, dilation * w)

    def spec(which, shift):
        return pl.BlockSpec((Q_BLOCK, hw), lambda c, n: (jnp.clip(n + shift, 0, nblk - 1), c * 3 + which))

    o, lse = pl.pallas_call(
        functools.partial(_dilated_body, nh=nh, dilation=dilation, radius=radius, seqs=seqs),
        grid=(dilation, nblk),
        in_specs=[spec(0, 0), spec(1, -1), spec(1, 0), spec(1, 1), spec(2, -1), spec(2, 0), spec(2, 1)],
        out_specs=[pl.BlockSpec((Q_BLOCK, hw), lambda c, n: (n, c)),
                   pl.BlockSpec((Q_BLOCK, LANES), lambda c, n: (n, c))],
        out_shape=[jax.ShapeDtypeStruct((rows, dilation * hw), F32),
                   jax.ShapeDtypeStruct((rows, dilation * LANES), F32)],
        compiler_params=_params("parallel", "parallel"),
        name=f"dilated_d{dilation}",
    )(view, view, view, view, view, view, view)
    return o.reshape(t, hw), lse.reshape(t, LANES)


def _merge_body(*refs, ng, nh):
    o_refs, lse_refs, out_ref = refs[:ng], refs[ng:2 * ng], refs[2 * ng]
    lses = [r[...] for r in lse_refs]
    mx = functools.reduce(jnp.maximum, lses)
    es = [jnp.exp2(x - mx) for x in lses]
    tot = functools.reduce(lambda a, b: a + b, es)
    ws = [e / tot for e in es]
    for h in range(nh):
        sl = slice(h * HEAD_DIM, (h + 1) * HEAD_DIM)
        acc = ws[0][:, h:h + 1] * o_refs[0][:, sl]
        for gi in range(1, ng):
            acc = acc + ws[gi][:, h:h + 1] * o_refs[gi][:, sl]
        out_ref[:, sl] = acc.astype(out_ref.dtype)


def merge_groups(outs, lses, nh):
    t, hw = outs[0].shape
    ng = len(outs)
    bm = _pick(t, 256, 8)
    return pl.pallas_call(
        functools.partial(_merge_body, ng=ng, nh=nh),
        grid=(t // bm,),
        in_specs=[pl.BlockSpec((bm, hw), lambda i: (i, 0))] * ng + [pl.BlockSpec((bm, LANES), lambda i: (i, 0))] * ng,
        out_specs=pl.BlockSpec((bm, hw), lambda i: (i, 0)),
        out_shape=jax.ShapeDtypeStruct((t, hw), BF16),
        compiler_params=_params("parallel"),
        name="dilated_merge",
    )(*outs, *lses)


def _na_block_maps(seqs):
    def maps(rb):
        r0 = rb * NA_BLOCK_ROWS
        start, nrows = seqs.bounds(r0, GRID_W)
        rl = r0 - start
        ws = start + jnp.clip(rl - NA_ROWS // 2, 0, nrows - NA_WIN_ROWS)
        variant = jnp.where(rl == 0, 0, jnp.where(rl == nrows - NA_BLOCK_ROWS, 2, 1))
        return ws, variant

    return maps


def _na_body(q_ref, k_ref, v_ref, b_ref, o_ref, *, nh):
    sls = [slice(h * HEAD_DIM, (h + 1) * HEAD_DIM) for h in range(nh)]
    ss = [lax.dot_general(q_ref[:, sl], k_ref[:, sl], NT_DIMS, preferred_element_type=F32) + b_ref[0, h]
          for h, sl in enumerate(sls)]
    for sl, s in zip(sls, ss):
        m = jnp.max(s, axis=-1, keepdims=True)
        p = jnp.exp2(s - m)
        l = jnp.sum(p, axis=-1, keepdims=True)
        o_ref[:, sl] = (jnp.dot(p.astype(BF16), v_ref[:, sl], preferred_element_type=F32) / l).astype(o_ref.dtype)


def na_bias_tables(rpb):
    c = jnp.arange(GRID_W)
    cs = jnp.clip(c - NA_COLS // 2, 0, GRID_W - NA_COLS)
    col_ok = (c[None, :] >= cs[:, None]) & (c[None, :] < cs[:, None] + NA_COLS)
    col_idx = jnp.clip(c[None, :] - c[:, None] + NA_COLS - 1, 0, 2 * NA_COLS - 2)
    rpb_c = jnp.where(col_ok[None, None], rpb.astype(F32)[:, :, col_idx] * LOG2E, MASK_VALUE)
    q = np.arange(NA_BLOCK_ROWS)
    half = NA_ROWS // 2
    variants = [(np.zeros_like(q), q - half),
                (q, np.zeros_like(q)),
                (np.full_like(q, NA_WIN_ROWS - NA_ROWS), q)]
    kr = np.arange(NA_WIN_ROWS)
    tabs = []
    for off, e in variants:
        rr = kr[None, :] - off[:, None]
        valid = (rr >= 0) & (rr < NA_ROWS)
        row_off = np.clip(rr + half - 1 - e[:, None], 0, 2 * NA_ROWS - 2)
        tab = jnp.where(jnp.asarray(valid)[None, :, :, None, None], rpb_c[:, row_off], MASK_VALUE)
        tabs.append(jnp.transpose(tab, (0, 1, 3, 2, 4)).reshape(rpb.shape[0], NA_BLOCK_ROWS * GRID_W,
                                                               NA_WIN_ROWS * GRID_W))
    return jnp.stack(tabs)


def neighbourhood_attention(qkv, bias, nh, seqs):
    t, w = qkv.shape
    hw = nh * HEAD_DIM
    hps = min(NA_HEADS_PER_STEP, nh)
    gw = hps * HEAD_DIM
    assert seqs.s2 % (NA_BLOCK_ROWS * GRID_W) == 0 and seqs.s2 >= NA_WIN_ROWS * GRID_W
    maps = _na_block_maps(seqs)
    bq = NA_BLOCK_ROWS * GRID_W
    kw = NA_WIN_ROWS * GRID_W

    def win_spec(col0):
        return pl.BlockSpec((pl.Element(kw), pl.Element(gw)),
                            lambda g, rb: (pl.multiple_of(maps(rb)[0] * GRID_W, GRID_W), pl.multiple_of(col0 + g * gw, LANES)))

    return pl.pallas_call(
        functools.partial(_na_body, nh=hps),
        grid=(nh // hps, t // bq),
        in_specs=[pl.BlockSpec((bq, gw), lambda g, rb: (rb, g)),
                  win_spec(hw), win_spec(2 * hw),
                  pl.BlockSpec((1, hps, bq, kw), lambda g, rb: (maps(rb)[1], g, 0, 0))],
        out_specs=pl.BlockSpec((bq, gw), lambda g, rb: (rb, g)),
        out_shape=jax.ShapeDtypeStruct((t, hw), BF16),
        compiler_params=_params("parallel", "arbitrary"),
        name="neighbourhood_attention",
    )(qkv, qkv, qkv, bias)


def _fold_qkv(w, gain, widths, scale):
    parts, c0 = [], 0
    for is_q, wd in widths:
        blk = w[:, c0:c0 + wd]
        parts.append(blk * scale if is_q else blk)
        c0 += wd
    return (jnp.concatenate(parts, axis=1) * gain.astype(F32)[:, None]).astype(BF16)


def _pad_cols(w, n):
    return jnp.concatenate([w, jnp.zeros(w.shape[:-1] + (n,), w.dtype)], axis=-1) if n else w


def kernel(x_prompt, x_sample, norm_mix, norm_ffn, norm_final, a_w_qkv, a_q_gain, a_k_gain, a_w_o, b_w_qkv,
           b_lambda_q1, b_lambda_k1, b_lambda_q2, b_lambda_k2, b_subln_gain, b_w_o, c_w_qkv, c_w_o, d_w_qkv, d_rpb,
           d_w_o, ffn_w_gate, ffn_w_up, ffn_w_down):
    b1, s1, d = x_prompt.shape
    b2, s2, _ = x_sample.shape
    seqs = Seqs(b1, s1, b2, s2)
    depth = norm_mix.shape[0]
    x = jnp.concatenate([x_sample.reshape(b2 * s2, d), x_prompt.reshape(b1 * s1, d)], axis=0)

    a_nk = (a_w_qkv.shape[1] - d) // (2 * HEAD_DIM)
    a_nq = d // HEAD_DIM
    b_nh = b_w_o.shape[0] // (2 * HEAD_DIM)
    c_nh = c_w_o.shape[0] // HEAD_DIM
    d_nh = d_w_o.shape[0] // HEAD_DIM
    ng = len(C_CONFIGS)
    chw = c_nh * HEAD_DIM
    qscale = LOG2E * HEAD_DIM ** -0.5
    qkv_cols = {0: [(False, a_w_qkv.shape[1])],
                1: [(True, b_nh * 2 * HEAD_DIM), (False, 2 * b_nh * 2 * HEAD_DIM)],
                2: [(True, chw), (False, 2 * chw)] * ng,
                3: [(True, d_nh * HEAD_DIM), (False, 2 * d_nh * HEAD_DIM)]}
    w_qkv = {0: a_w_qkv, 1: b_w_qkv, 2: c_w_qkv, 3: d_w_qkv}
    w_o = {0: a_w_o, 1: b_w_o, 2: c_w_o, 3: d_w_o}
    dff = ffn_w_gate.shape[2]
    pad = (-dff) % FF_ALIGN

    cos, sin = rope_tables(seqs)
    slopes_b = 2.0 ** (-8.0 * jnp.arange(1, b_nh + 1, dtype=F32) / b_nh)
    na_bias = na_bias_tables(d_rpb)

    xb, rstd = prep_norm(x)
    for i in range(depth):
        kind = i % 4
        wq = _fold_qkv(w_qkv[kind], norm_mix[i], qkv_cols[kind], qscale)
        if kind == 0:
            qkv = matmul_scaled(xb, wq, rstd, F32)
            q, k, v1 = rope_qk(qkv, cos, sin, a_q_gain, a_k_gain, a_nq, a_nk)
            o = gqa_attention(q, k, v1, seqs)
        elif kind == 1:
            qkv = matmul_scaled(xb, wq, rstd, BF16)
            lambda_init = 0.8 - 0.6 * math.exp(-0.3 * i)
            o = diff_attention(qkv, slopes_b, b_lambda_q1, b_lambda_k1, b_lambda_q2, b_lambda_k2, b_subln_gain, seqs,
                               lambda_init)
        elif kind == 2:
            outs, lses = zip(*[dilated_group(matmul_scaled(xb, wq[:, g * 3 * chw:(g + 1) * 3 * chw], rstd, BF16),
                                             c_nh, win, dil, seqs) for g, (win, dil) in enumerate(C_CONFIGS)])
            o = merge_groups(outs, lses, c_nh)
        else:
            qkv = matmul_scaled(xb, wq, rstd, BF16)
            o = neighbourhood_attention(qkv, na_bias, d_nh, seqs)
        x, xb, rstd = matmul_res(o, w_o[kind].astype(BF16), x)
        gain = norm_ffn[i].astype(F32)[:, None]
        w_gate = _pad_cols((ffn_w_gate[i] * gain).astype(BF16), pad)
        w_up = _pad_cols((ffn_w_up[i] * gain).astype(BF16), pad)
        w_down = jnp.concatenate([ffn_w_down[i].astype(BF16), jnp.zeros((pad, d), BF16)], axis=0) if pad else \
            ffn_w_down[i].astype(BF16)
        a = gateup(xb, w_gate, w_up, rstd)
        x, xb, rstd = matmul_kgrid_res(a, w_down, x)

    y_sample = rmsnorm(x, norm_final, F32, row0=0, rows=b2 * s2).reshape(b2, s2, d)
    y_prompt = rmsnorm(x, norm_final, F32, row0=b2 * s2, rows=b1 * s1).reshape(b1, s1, d)
    return y_prompt, y_sample
```

```python
import functools
import math

import jax
import jax.numpy as jnp
import numpy as np
from jax import lax
from jax.experimental import pallas as pl
from jax.experimental.pallas import tpu as pltpu

HEAD_DIM = 128
GRID_W = 64
Q_BLOCK = 128
RMS_EPS = 1e-6
MASK_VALUE = -1e30
ROPE_THETA = 10000.0
C_CONFIGS = ((128, 1), (512, 4), (2048, 16))
NA_ROWS = 8
NA_COLS = 16
NA_BLOCK_ROWS = 4
NA_WIN_ROWS = NA_BLOCK_ROWS + NA_ROWS
NA_HEADS_PER_STEP = 8
LOG2E = math.log2(math.e)
LANES = 128
V7X_VMEM_LIMIT_BYTES = 56 * 1024 * 1024

F32 = jnp.float32
BF16 = jnp.bfloat16
NT_DIMS = (((1,), (1,)), ((), ()))


def _params(*sem):
    return pltpu.CompilerParams(dimension_semantics=sem, vmem_limit_bytes=V7X_VMEM_LIMIT_BYTES)


def _pick(n, pref, align=LANES):
    if n <= pref:
        return n
    b = (pref // align) * align
    while b >= align:
        if n % b == 0:
            return b
        b -= align
    raise ValueError(f"no block for {n} under {pref}")


def _rmsnorm_body(x_ref, g_ref, o_ref):
    x = x_ref[...]
    ms = jnp.mean(x * x, axis=-1, keepdims=True)
    o_ref[...] = (x * lax.rsqrt(ms + RMS_EPS) * g_ref[...]).astype(o_ref.dtype)


def rmsnorm(x, gain, out_dtype, row0=0, rows=None):
    t, d = x.shape
    rows = t if rows is None else rows
    bm = _pick(math.gcd(rows, row0) if row0 else rows, 256, 8)
    off = row0 // bm
    return pl.pallas_call(
        _rmsnorm_body,
        grid=(rows // bm,),
        in_specs=[pl.BlockSpec((bm, d), lambda i: (i + off, 0)),
                  pl.BlockSpec((1, d), lambda i: (0, 0))],
        out_specs=pl.BlockSpec((bm, d), lambda i: (i, 0)),
        out_shape=jax.ShapeDtypeStruct((rows, d), out_dtype),
        compiler_params=_params("parallel"),
        name="rmsnorm",
    )(x, gain.reshape(1, d).astype(F32))


def _row_scale(acc, rstd):
    return jnp.concatenate([acc[:, c:c + LANES] * rstd for c in range(0, acc.shape[1], LANES)], axis=1)


def _lane_partial_sumsq(x):
    return sum(x[:, c:c + LANES] * x[:, c:c + LANES] for c in range(0, x.shape[1], LANES))


def _finish_rstd(ssq, d_model):
    total = jnp.sum(ssq, axis=-1, keepdims=True)
    return jnp.broadcast_to(lax.rsqrt(total / d_model + RMS_EPS), ssq.shape)


def _prep_body(x_ref, xb_ref, rstd_ref):
    x = x_ref[...]
    xb_ref[...] = x.astype(xb_ref.dtype)
    rstd_ref[...] = _finish_rstd(_lane_partial_sumsq(x), x.shape[1])


def prep_norm(x):
    t, d = x.shape
    bm = _pick(t, 256, 8)
    return pl.pallas_call(
        _prep_body,
        grid=(t // bm,),
        in_specs=[pl.BlockSpec((bm, d), lambda i: (i, 0))],
        out_specs=[pl.BlockSpec((bm, d), lambda i: (i, 0)), pl.BlockSpec((bm, LANES), lambda i: (i, 0))],
        out_shape=[jax.ShapeDtypeStruct((t, d), BF16), jax.ShapeDtypeStruct((t, LANES), F32)],
        compiler_params=_params("parallel"),
        name="prep_norm",
    )(x)


def _mm_scaled_body(a_ref, b_ref, s_ref, o_ref):
    acc = jnp.dot(a_ref[...], b_ref[...], preferred_element_type=F32)
    o_ref[...] = _row_scale(acc, s_ref[...]).astype(o_ref.dtype)


def matmul_scaled(a, b, rstd, out_dtype, bm_pref=1024, bn_pref=512):
    m, k = a.shape
    _, n = b.shape
    bm, bn = _pick(m, bm_pref, 8), _pick(n, bn_pref)
    return pl.pallas_call(
        _mm_scaled_body,
        grid=(m // bm, n // bn),
        in_specs=[pl.BlockSpec((bm, k), lambda i, j: (i, 0)),
                  pl.BlockSpec((k, bn), lambda i, j: (0, j)),
                  pl.BlockSpec((bm, LANES), lambda i, j: (i, 0))],
        out_specs=pl.BlockSpec((bm, bn), lambda i, j: (i, j)),
        out_shape=jax.ShapeDtypeStruct((m, n), out_dtype),
        compiler_params=_params("parallel", "parallel"),
        name="matmul",
    )(a, b, rstd)


def _emit_stream(x, first, last, d_model, o_ref, xb_ref, rstd_ref):
    o_ref[...] = x
    xb_ref[...] = x.astype(xb_ref.dtype)
    part = _lane_partial_sumsq(x)

    @pl.when(first)
    def _():
        rstd_ref[...] = part

    @pl.when(jnp.logical_not(first))
    def _():
        rstd_ref[...] += part

    @pl.when(last)
    def _():
        rstd_ref[...] = _finish_rstd(rstd_ref[...], d_model)


def _stream_out(m, n, bm, bn, idx):
    specs = [pl.BlockSpec((bm, bn), idx), pl.BlockSpec((bm, bn), idx),
             pl.BlockSpec((bm, LANES), lambda i, *_: (i, 0))]
    shapes = [jax.ShapeDtypeStruct((m, n), F32), jax.ShapeDtypeStruct((m, n), BF16),
              jax.ShapeDtypeStruct((m, LANES), F32)]
    return specs, shapes


def _mm_res_body(a_ref, b_ref, r_ref, o_ref, xb_ref, rstd_ref, *, d_model):
    j = pl.program_id(1)
    x = jnp.dot(a_ref[...], b_ref[...], preferred_element_type=F32) + r_ref[...]
    _emit_stream(x, j == 0, j == pl.num_programs(1) - 1, d_model, o_ref, xb_ref, rstd_ref)


def matmul_res(a, b, residual, bm_pref=1024, bn_pref=512):
    m, k = a.shape
    _, n = b.shape
    bm, bn = _pick(m, bm_pref, 8), _pick(n, bn_pref)
    out_specs, out_shape = _stream_out(m, n, bm, bn, lambda i, j: (i, j))
    return pl.pallas_call(
        functools.partial(_mm_res_body, d_model=n),
        grid=(m // bm, n // bn),
        in_specs=[pl.BlockSpec((bm, k), lambda i, j: (i, 0)),
                  pl.BlockSpec((k, bn), lambda i, j: (0, j)),
                  pl.BlockSpec((bm, bn), lambda i, j: (i, j))],
        out_specs=out_specs,
        out_shape=out_shape,
        compiler_params=_params("parallel", "arbitrary"),
        name="matmul_res",
    )(a, b, residual)


def _gateup_body(h_ref, wg_ref, wu_ref, s_ref, o_ref):
    h = h_ref[...]
    rstd = s_ref[...]
    g = _row_scale(jnp.dot(h, wg_ref[...], preferred_element_type=F32), rstd)
    u = _row_scale(jnp.dot(h, wu_ref[...], preferred_element_type=F32), rstd)
    o_ref[...] = (g / (1.0 + jnp.exp(-g)) * u).astype(o_ref.dtype)


def gateup(h, wg, wu, rstd, bm_pref=1024, bn_pref=512):
    m, k = h.shape
    _, n = wg.shape
    bm, bn = _pick(m, bm_pref, 8), min(bn_pref, n)
    return pl.pallas_call(
        _gateup_body,
        grid=(m // bm, pl.cdiv(n, bn)),
        in_specs=[pl.BlockSpec((bm, k), lambda i, j: (i, 0)),
                  pl.BlockSpec((k, bn), lambda i, j: (0, j)),
                  pl.BlockSpec((k, bn), lambda i, j: (0, j)),
                  pl.BlockSpec((bm, LANES), lambda i, j: (i, 0))],
        out_specs=pl.BlockSpec((bm, bn), lambda i, j: (i, j)),
        out_shape=jax.ShapeDtypeStruct((m, n), BF16),
        compiler_params=_params("parallel", "parallel"),
        name="gateup",
    )(h, wg, wu, rstd)


class Seqs:
    def __init__(self, b1, s1, b2, s2):
        assert b1 == 1 and s1 == 2 * s2, "layout assumes one prompt of twice the sample length"
        self.s1, self.s2, self.nb2 = s1, s2, b2
        self.p0 = b2 * s2
        self.t = self.p0 + s1

    def positions(self):
        return jnp.concatenate([jnp.arange(self.s2)] * self.nb2 + [jnp.arange(self.s1)])

    def bounds(self, r0, unit=1):
        p0, s1, s2 = self.p0 // unit, self.s1 // unit, self.s2 // unit
        is_p = r0 >= p0
        return jnp.where(is_p, p0, r0 // s2 * s2), jnp.where(is_p, s1, s2)


def _swap_quarters(y):
    lane = lax.broadcasted_iota(jnp.int32, y.shape, 1)
    first = (lane % (HEAD_DIM // 2)) < (HEAD_DIM // 4)
    return jnp.where(first, pltpu.roll(y, HEAD_DIM - HEAD_DIM // 4, 1), pltpu.roll(y, HEAD_DIM // 4, 1))


def _rope_body(x_ref, cos_ref, sin_ref, qg_ref, kg_ref, q_ref, k_ref, v_ref, *, nq, nk, scale):
    cos, sin = cos_ref[...], sin_ref[...]

    def norm_rope(x, gain):
        ms = jnp.mean(x * x, axis=-1, keepdims=True)
        y = x * lax.rsqrt(ms + RMS_EPS) * gain
        return y * cos + _swap_quarters(y) * sin

    for h in range(nq):
        sl = slice(h * HEAD_DIM, (h + 1) * HEAD_DIM)
        q_ref[:, sl] = (norm_rope(x_ref[:, sl], qg_ref[...]) * scale).astype(q_ref.dtype)
    ones = jnp.ones((x_ref.shape[0], HEAD_DIM), v_ref.dtype)
    for h in range(nk):
        src = slice((nq + h) * HEAD_DIM, (nq + h + 1) * HEAD_DIM)
        k_ref[:, h * HEAD_DIM:(h + 1) * HEAD_DIM] = norm_rope(x_ref[:, src], kg_ref[...]).astype(k_ref.dtype)
        vsrc = slice((nq + nk + h) * HEAD_DIM, (nq + nk + h + 1) * HEAD_DIM)
        v_ref[:, 2 * h * HEAD_DIM:(2 * h + 1) * HEAD_DIM] = x_ref[:, vsrc].astype(v_ref.dtype)
        v_ref[:, (2 * h + 1) * HEAD_DIM:(2 * h + 2) * HEAD_DIM] = ones


def rope_qk(qkv, cos, sin, q_gain, k_gain, nq, nk):
    t, w = qkv.shape
    bm = _pick(t, 256, 8)
    row = lambda i: (i, 0)
    fixed = lambda i: (0, 0)
    return pl.pallas_call(
        functools.partial(_rope_body, nq=nq, nk=nk, scale=LOG2E * HEAD_DIM ** -0.5),
        grid=(t // bm,),
        in_specs=[pl.BlockSpec((bm, w), row), pl.BlockSpec((bm, HEAD_DIM), row), pl.BlockSpec((bm, HEAD_DIM), row),
                  pl.BlockSpec((1, HEAD_DIM), fixed), pl.BlockSpec((1, HEAD_DIM), fixed)],
        out_specs=[pl.BlockSpec((bm, nq * HEAD_DIM), row), pl.BlockSpec((bm, nk * HEAD_DIM), row),
                   pl.BlockSpec((bm, 2 * nk * HEAD_DIM), row)],
        out_shape=[jax.ShapeDtypeStruct((t, nq * HEAD_DIM), BF16), jax.ShapeDtypeStruct((t, nk * HEAD_DIM), BF16),
                   jax.ShapeDtypeStruct((t, 2 * nk * HEAD_DIM), BF16)],
        compiler_params=_params("parallel"),
        name="rope_qk",
    )(qkv, cos, sin, q_gain.reshape(1, -1).astype(F32), k_gain.reshape(1, -1).astype(F32))


def rope_tables(seqs):
    half = HEAD_DIM // 2
    inv = ROPE_THETA ** (-jnp.arange(0, half, 2, dtype=F32) / half)
    t = seqs.positions()
    ang_r = (t // GRID_W).astype(F32)[:, None] * inv
    ang_c = (t % GRID_W).astype(F32)[:, None] * inv
    cr, sr, cc, sc = jnp.cos(ang_r), jnp.sin(ang_r), jnp.cos(ang_c), jnp.sin(ang_c)
    return jnp.concatenate([cr, cr, cc, cc], axis=-1), jnp.concatenate([-sr, sr, -sc, sc], axis=-1)


def _kv_window_spec(seqs, bq, width, col0):
    return pl.BlockSpec((pl.Element(seqs.s1), pl.Element(width)),
                        lambda h, i: (pl.multiple_of(seqs.bounds(i * bq)[0], bq), pl.multiple_of(col0(h), LANES)))


FLASH_ROW_CHUNK = 32


def _flash_pipeline(nblk, scores, probs, pv_scale, s_bufs, p_bufs):
    (s_e, s_o), (p_e, p_o) = s_bufs, p_bufs
    scores(0, s_e)
    scores(1, s_o)
    probs(0, s_e, p_e)

    def pair(jj, carry):
        j = 2 * jj + 1
        scores(j + 1, s_e)
        probs(j, s_o, p_o)
        pv_scale(j - 1, p_e, True)
        scores(j + 2, s_o)
        probs(j + 1, s_e, p_e)
        pv_scale(j, p_o, True)
        return carry

    lax.fori_loop(0, (nblk - 2) // 2, pair, 0)
    probs(nblk - 1, s_o, p_o)
    pv_scale(nblk - 2, p_e, True)
    pv_scale(nblk - 1, p_o, False)


def _flash_scratch(m_rows, bkv, acc_width, n_stats):
    return ([pltpu.VMEM((m_rows, bkv), F32)] * 2 + [pltpu.VMEM((m_rows, bkv), BF16)] * 2
            + [pltpu.VMEM((m_rows, acc_width), F32)] + [pltpu.VMEM((m_rows, 1), F32)] * n_stats)


def _gqa_body(q_ref, k_ref, v_ref, o_ref, s_e, s_o, p_e, p_o, acc_ref, m_ref, alpha_ref, *, rep, bq, bkv, seqs):
    i = pl.program_id(1)
    nblk = seqs.bounds(i * bq)[1] // bkv
    q = q_ref[...]
    qs = jnp.concatenate([q[:, r * HEAD_DIM:(r + 1) * HEAD_DIM] for r in range(rep)], axis=0)
    rows = lambda j: pl.ds(pl.multiple_of(j * bkv, bkv), bkv)

    def scores(j, s_ref):
        s_ref[...] = lax.dot_general(qs, k_ref[rows(j), :], NT_DIMS, preferred_element_type=F32)

    def probs(j, s_ref, p_ref):
        for r0 in range(0, rep * bq, FLASH_ROW_CHUNK):
            rs = slice(r0, r0 + FLASH_ROW_CHUNK)
            s = s_ref[rs, :]
            m = m_ref[rs, :]
            m_new = jnp.maximum(m, jnp.max(s, axis=-1, keepdims=True))
            p_ref[rs, :] = jnp.exp2(s - m_new).astype(BF16)
            alpha_ref[rs, :] = jnp.exp2(m - m_new)
            m_ref[rs, :] = m_new

    def pv_scale(j, p_ref, rescale):
        acc = acc_ref[...] + jnp.dot(p_ref[...], v_ref[rows(j), :], preferred_element_type=F32)
        acc_ref[...] = acc * alpha_ref[...] if rescale else acc

    acc_ref[...] = jnp.zeros_like(acc_ref)
    m_ref[...] = jnp.full_like(m_ref, MASK_VALUE)
    _flash_pipeline(nblk, scores, probs, pv_scale, (s_e, s_o), (p_e, p_o))
    acc = acc_ref[...]
    o = acc[:, :HEAD_DIM] / acc[:, HEAD_DIM:]
    for r in range(rep):
        o_ref[:, r * HEAD_DIM:(r + 1) * HEAD_DIM] = o[r * bq:(r + 1) * bq].astype(o_ref.dtype)


def gqa_attention(q, k, v1, seqs, bq_pref=256, bkv_pref=1024):
    t, wq = q.shape
    nk = k.shape[1] // HEAD_DIM
    rep = wq // HEAD_DIM // nk
    bq = _pick(seqs.s2, bq_pref, 16)
    bkv = _pick(seqs.s2 // 2, bkv_pref)
    return pl.pallas_call(
        functools.partial(_gqa_body, rep=rep, bq=bq, bkv=bkv, seqs=seqs),
        grid=(nk, t // bq),
        in_specs=[pl.BlockSpec((bq, rep * HEAD_DIM), lambda g, i: (i, g)),
                  _kv_window_spec(seqs, bq, HEAD_DIM, lambda g: g * HEAD_DIM),
                  _kv_window_spec(seqs, bq, 2 * HEAD_DIM, lambda g: g * 2 * HEAD_DIM)],
        out_specs=pl.BlockSpec((bq, rep * HEAD_DIM), lambda g, i: (i, g)),
        out_shape=jax.ShapeDtypeStruct((t, wq), BF16),
        scratch_shapes=_flash_scratch(rep * bq, bkv, 2 * HEAD_DIM, 2),
        compiler_params=_params("parallel", "arbitrary"),
        name="gqa_attention",
    )(q, k, v1)


def _diff_body(q_ref, k_ref, v_ref, slope_ref, tab_ref, lq1_ref, lk1_ref, lq2_ref, lk2_ref, g_ref, o_ref,
               s_e, s_o, p_e, p_o, acc_ref, m_ref, alpha_ref, l_ref, *, bq, bkv, seqs, lambda_init):
    i = pl.program_id(1)
    start, slen = seqs.bounds(i * bq)
    nblk = slen // bkv
    qpos0 = i * bq - start
    q = q_ref[...]
    q0, q1 = q[:, :HEAD_DIM], q[:, HEAD_DIM:]
    slope = slope_ref[0][:, :1] * LOG2E
    rows = lambda j: pl.ds(pl.multiple_of(j * bkv, bkv), bkv)

    def scores(j, s_ref):
        k = k_ref[rows(j), :]
        s_ref[:bq] = lax.dot_general(q0, k[:, :HEAD_DIM], NT_DIMS, preferred_element_type=F32)
        s_ref[bq:] = lax.dot_general(q1, k[:, HEAD_DIM:], NT_DIMS, preferred_element_type=F32)

    def probs(j, s_ref, p_ref):
        subs = []
        for c0 in range(0, bkv, bq):
            lead = qpos0 - (j * bkv + c0)
            kind = jnp.where(lead == 0, 2, jnp.where(lead > 0, 0, 1))
            subs.append((c0, kind, slope * jnp.abs(lead).astype(F32)))
        for r0 in range(0, 2 * bq, FLASH_ROW_CHUNK):
            rs = slice(r0, r0 + FLASH_ROW_CHUNK)
            rq = slice(r0 % bq, r0 % bq + FLASH_ROW_CHUNK)
            ss = [s_ref[rs, c0:c0 + bq] - tab_ref[0, kind, rq, :] for c0, kind, _ in subs]
            mx = functools.reduce(jnp.maximum, [jnp.max(s, axis=-1, keepdims=True) - shift
                                                for s, (_, _, shift) in zip(ss, subs)])
            m = m_ref[rs, :]
            m_new = jnp.maximum(m, mx)
            alpha = jnp.exp2(m - m_new)
            lsum = alpha * l_ref[rs, :]
            for s, (c0, _, shift) in zip(ss, subs):
                p = jnp.exp2(s - (m_new + shift))
                p_ref[rs, c0:c0 + bq] = p.astype(BF16)
                lsum = lsum + sum(p[:, c:c + LANES] for c in range(0, bq, LANES))
            l_ref[rs, :] = lsum
            alpha_ref[rs, :] = alpha
            m_ref[rs, :] = m_new

    def pv_scale(j, p_ref, rescale):
        acc = acc_ref[...] + jnp.dot(p_ref[...], v_ref[rows(j), :], preferred_element_type=F32)
        acc_ref[...] = acc * alpha_ref[...] if rescale else acc

    acc_ref[...] = jnp.zeros_like(acc_ref)
    m_ref[...] = jnp.full_like(m_ref, MASK_VALUE)
    l_ref[...] = jnp.zeros_like(l_ref)
    _flash_pipeline(nblk, scores, probs, pv_scale, (s_e, s_o), (p_e, p_o))
    o = acc_ref[...] / jnp.sum(l_ref[...], axis=-1, keepdims=True)
    lam = (jnp.exp(jnp.sum(lq1_ref[...] * lk1_ref[...], axis=-1, keepdims=True))
           - jnp.exp(jnp.sum(lq2_ref[...] * lk2_ref[...], axis=-1, keepdims=True)) + lambda_init)
    d = o[:bq] - lam * o[bq:]
    ms = jnp.mean(d * d, axis=-1, keepdims=True)
    o_ref[...] = (d * lax.rsqrt(ms + RMS_EPS) * g_ref[...] * (1.0 - lambda_init)).astype(o_ref.dtype)


def diff_attention(qkv, slopes, lq1, lk1, lq2, lk2, subln_gain, seqs, lambda_init, bq_pref=512, bkv_pref=1024):
    t, w = qkv.shape
    hw = 2 * HEAD_DIM
    nh = w // (3 * hw)
    bq = _pick(seqs.s2 // 2, bq_pref)
    bkv = _pick(seqs.s2 // 2, bkv_pref)
    assert bkv % bq == 0
    rel = (jnp.arange(bq)[:, None] - jnp.arange(bq)[None, :]).astype(F32) * (LOG2E * slopes.astype(F32))[:, None, None]
    tabs = jnp.stack([rel, -rel, jnp.abs(rel)], axis=1)
    vec = lambda a: a.reshape(1, -1).astype(F32)
    vec_spec = lambda n: pl.BlockSpec((1, n), lambda h, i: (0, 0))
    return pl.pallas_call(
        functools.partial(_diff_body, bq=bq, bkv=bkv, seqs=seqs, lambda_init=lambda_init),
        grid=(nh, t // bq),
        in_specs=[pl.BlockSpec((bq, hw), lambda h, i: (i, h)),
                  _kv_window_spec(seqs, bq, hw, lambda h: (nh + h) * hw),
                  _kv_window_spec(seqs, bq, hw, lambda h: (2 * nh + h) * hw),
                  pl.BlockSpec((1, 1, LANES), lambda h, i: (h, 0, 0)),
                  pl.BlockSpec((1, 3, bq, bq), lambda h, i: (h, 0, 0, 0)),
                  vec_spec(HEAD_DIM), vec_spec(HEAD_DIM), vec_spec(HEAD_DIM), vec_spec(HEAD_DIM), vec_spec(hw)],
        out_specs=pl.BlockSpec((bq, hw), lambda h, i: (i, h)),
        out_shape=jax.ShapeDtypeStruct((t, nh * hw), BF16),
        scratch_shapes=_flash_scratch(2 * bq, bkv, hw, 2) + [pltpu.VMEM((2 * bq, LANES), F32)],
        compiler_params=_params("parallel", "arbitrary"),
        name="diff_attention",
    )(qkv, qkv, qkv, jnp.broadcast_to(slopes.astype(F32)[:, None, None], (nh, 1, LANES)), tabs,
      vec(lq1), vec(lk1), vec(lq2), vec(lk2), vec(subln_gain))


def _dilated_body(q_ref, kp_ref, kc_ref, kn_ref, vp_ref, vc_ref, vn_ref, o_ref, lse_ref, *, nh, dilation, radius, seqs):
    n = pl.program_id(1)
    qb = Q_BLOCK
    row0 = n * qb
    seq_start, seq_len = seqs.bounds(row0, dilation)
    kw = qb + 2 * radius
    r_i = lax.broadcasted_iota(jnp.int32, (qb, kw), 0)
    c_i = lax.broadcasted_iota(jnp.int32, (qb, kw), 1)
    jrel = c_i - radius - r_i
    kabs = row0 - radius + c_i
    valid = (jnp.abs(jrel) <= radius) & (kabs >= seq_start) & (kabs < seq_start + seq_len)
    dist = (dilation * jnp.abs(jrel)).astype(F32)
    lane = lax.broadcasted_iota(jnp.int32, (qb, LANES), 1)
    lse_tile = jnp.zeros((qb, LANES), F32)
    for h in range(nh):
        sl = slice(h * HEAD_DIM, (h + 1) * HEAD_DIM)
        slope = LOG2E * 2.0 ** (-8.0 * (h + 1) / nh)
        k = jnp.concatenate([kp_ref[qb - radius:, sl], kc_ref[:, sl], kn_ref[:radius, sl]], axis=0)
        v = jnp.concatenate([vp_ref[qb - radius:, sl], vc_ref[:, sl], vn_ref[:radius, sl]], axis=0)
        s = lax.dot_general(q_ref[:, sl], k, NT_DIMS, preferred_element_type=F32)
        s = jnp.where(valid, s - slope * dist, MASK_VALUE)
        m = jnp.max(s, axis=-1, keepdims=True)
        p = jnp.exp2(s - m)
        l = jnp.sum(p, axis=-1, keepdims=True)
        o_ref[:, sl] = jnp.dot(p.astype(BF16), v, preferred_element_type=F32) / l
        lse_tile = jnp.where(lane == h, m + jnp.log2(l), lse_tile)
    lse_ref[...] = lse_tile


def dilated_group(qkv, nh, window, dilation, seqs):
    t, w = qkv.shape
    hw = nh * HEAD_DIM
    radius = window // (2 * dilation)
    assert radius <= Q_BLOCK and seqs.s2 % (dilation * Q_BLOCK) == 0
    rows = t // dilation
    nblk = rows // Q_BLOCK
    view = qkv.reshape(rows, dilation * w)

    def spec(which, shift):
        return pl.BlockSpec((Q_BLOCK, hw), lambda c, n: (jnp.clip(n + shift, 0, nblk - 1), c * 3 + which))

    o, lse = pl.pallas_call(
        functools.partial(_dilated_body, nh=nh, dilation=dilation, radius=radius, seqs=seqs),
        grid=(dilation, nblk),
        in_specs=[spec(0, 0), spec(1, -1), spec(1, 0), spec(1, 1), spec(2, -1), spec(2, 0), spec(2, 1)],
        out_specs=[pl.BlockSpec((Q_BLOCK, hw), lambda c, n: (n, c)),
                   pl.BlockSpec((Q_BLOCK, LANES), lambda c, n: (n, c))],
        out_shape=[jax.ShapeDtypeStruct((rows, dilation * hw), F32),
                   jax.ShapeDtypeStruct((rows, dilation * LANES), F32)],
        compiler_params=_params("parallel", "parallel"),
        name=f"dilated_d{dilation}",
    )(view, view, view, view, view, view, view)
    return o.reshape(t, hw), lse.reshape(t, LANES)


def _merge_body(*refs, ng, nh):
    o_refs, lse_refs, out_ref = refs[:ng], refs[ng:2 * ng], refs[2 * ng]
    lses = [r[...] for r in lse_refs]
    mx = functools.reduce(jnp.maximum, lses)
    es = [jnp.exp2(x - mx) for x in lses]
    tot = functools.reduce(lambda a, b: a + b, es)
    ws = [e / tot for e in es]
    for h in range(nh):
        sl = slice(h * HEAD_DIM, (h + 1) * HEAD_DIM)
        acc = ws[0][:, h:h + 1] * o_refs[0][:, sl]
        for gi in range(1, ng):
            acc = acc + ws[gi][:, h:h + 1] * o_refs[gi][:, sl]
        out_ref[:, sl] = acc.astype(out_ref.dtype)


def merge_groups(outs, lses, nh):
    t, hw = outs[0].shape
    ng = len(outs)
    bm = _pick(t, 256, 8)
    return pl.pallas_call(
        functools.partial(_merge_body, ng=ng, nh=nh),
        grid=(t // bm,),
        in_specs=[pl.BlockSpec((bm, hw), lambda i: (i, 0))] * ng + [pl.BlockSpec((bm, LANES), lambda i: (i, 0))] * ng,
        out_specs=pl.BlockSpec((bm, hw), lambda i: (i, 0)),
        out_shape=jax.ShapeDtypeStruct((t, hw), BF16),
        compiler_params=_params("parallel"),
        name="dilated_merge",
    )(*outs, *lses)


def _na_block_maps(seqs):
    def maps(rb):
        r0 = rb * NA_BLOCK_ROWS
        start, nrows = seqs.bounds(r0, GRID_W)
        rl = r0 - start
        ws = start + jnp.clip(rl - NA_ROWS // 2, 0, nrows - NA_WIN_ROWS)
        variant = jnp.where(rl == 0, 0, jnp.where(rl == nrows - NA_BLOCK_ROWS, 2, 1))
        return ws, variant

    return maps


def _na_body(q_ref, k_ref, v_ref, b_ref, o_ref, *, nh):
    sls = [slice(h * HEAD_DIM, (h + 1) * HEAD_DIM) for h in range(nh)]
    ss = [lax.dot_general(q_ref[:, sl], k_ref[:, sl], NT_DIMS, preferred_element_type=F32) + b_ref[0, h]
          for h, sl in enumerate(sls)]
    for sl, s in zip(sls, ss):
        m = jnp.max(s, axis=-1, keepdims=True)
        p = jnp.exp2(s - m)
        l = jnp.sum(p, axis=-1, keepdims=True)
        o_ref[:, sl] = (jnp.dot(p.astype(BF16), v_ref[:, sl], preferred_element_type=F32) / l).astype(o_ref.dtype)


def na_bias_tables(rpb):
    c = jnp.arange(GRID_W)
    cs = jnp.clip(c - NA_COLS // 2, 0, GRID_W - NA_COLS)
    col_ok = (c[None, :] >= cs[:, None]) & (c[None, :] < cs[:, None] + NA_COLS)
    col_idx = jnp.clip(c[None, :] - c[:, None] + NA_COLS - 1, 0, 2 * NA_COLS - 2)
    rpb_c = jnp.where(col_ok[None, None], rpb.astype(F32)[:, :, col_idx] * LOG2E, MASK_VALUE)
    q = np.arange(NA_BLOCK_ROWS)
    half = NA_ROWS // 2
    variants = [(np.zeros_like(q), q - half),
                (q, np.zeros_like(q)),
                (np.full_like(q, NA_WIN_ROWS - NA_ROWS), q)]
    kr = np.arange(NA_WIN_ROWS)
    tabs = []
    for off, e in variants:
        rr = kr[None, :] - off[:, None]
        valid = (rr >= 0) & (rr < NA_ROWS)
        row_off = np.clip(rr + half - 1 - e[:, None], 0, 2 * NA_ROWS - 2)
        tab = jnp.where(jnp.asarray(valid)[None, :, :, None, None], rpb_c[:, row_off], MASK_VALUE)
        tabs.append(jnp.transpose(tab, (0, 1, 3, 2, 4)).reshape(rpb.shape[0], NA_BLOCK_ROWS * GRID_W,
                                                               NA_WIN_ROWS * GRID_W))
    return jnp.stack(tabs)


def neighbourhood_attention(qkv, bias, nh, seqs):
    t, w = qkv.shape
    hw = nh * HEAD_DIM
    hps = min(NA_HEADS_PER_STEP, nh)
    gw = hps * HEAD_DIM
    assert seqs.s2 % (NA_BLOCK_ROWS * GRID_W) == 0 and seqs.s2 >= NA_WIN_ROWS * GRID_W
    maps = _na_block_maps(seqs)
    bq = NA_BLOCK_ROWS * GRID_W
    kw = NA_WIN_ROWS * GRID_W

    def win_spec(col0):
        return pl.BlockSpec((pl.Element(kw), pl.Element(gw)),
                            lambda g, rb: (pl.multiple_of(maps(rb)[0] * GRID_W, GRID_W), pl.multiple_of(col0 + g * gw, LANES)))

    return pl.pallas_call(
        functools.partial(_na_body, nh=hps),
        grid=(nh // hps, t // bq),
        in_specs=[pl.BlockSpec((bq, gw), lambda g, rb: (rb, g)),
                  win_spec(hw), win_spec(2 * hw),
                  pl.BlockSpec((1, hps, bq, kw), lambda g, rb: (maps(rb)[1], g, 0, 0))],
        out_specs=pl.BlockSpec((bq, gw), lambda g, rb: (rb, g)),
        out_shape=jax.ShapeDtypeStruct((t, hw), BF16),
        compiler_params=_params("parallel", "arbitrary"),
        name="neighbourhood_attention",
    )(qkv, qkv, qkv, bias)


def _fold_qkv(w, gain, widths, scale):
    col_scale = np.concatenate([np.full((wd,), scale if is_q else 1.0, np.float32) for is_q, wd in widths])
    return (w * (gain.astype(F32)[:, None] * col_scale[None, :])).astype(BF16)


def kernel(x_prompt, x_sample, norm_mix, norm_ffn, norm_final, a_w_qkv, a_q_gain, a_k_gain, a_w_o, b_w_qkv,
           b_lambda_q1, b_lambda_k1, b_lambda_q2, b_lambda_k2, b_subln_gain, b_w_o, c_w_qkv, c_w_o, d_w_qkv, d_rpb,
           d_w_o, ffn_w_gate, ffn_w_up, ffn_w_down):
    b1, s1, d = x_prompt.shape
    b2, s2, _ = x_sample.shape
    seqs = Seqs(b1, s1, b2, s2)
    depth = norm_mix.shape[0]
    x = jnp.concatenate([x_sample.reshape(b2 * s2, d), x_prompt.reshape(b1 * s1, d)], axis=0)

    a_nk = (a_w_qkv.shape[1] - d) // (2 * HEAD_DIM)
    a_nq = d // HEAD_DIM
    b_nh = b_w_o.shape[0] // (2 * HEAD_DIM)
    c_nh = c_w_o.shape[0] // HEAD_DIM
    d_nh = d_w_o.shape[0] // HEAD_DIM
    ng = len(C_CONFIGS)
    chw = c_nh * HEAD_DIM
    qscale = LOG2E * HEAD_DIM ** -0.5
    qkv_cols = {0: [(False, a_w_qkv.shape[1])],
                1: [(True, b_nh * 2 * HEAD_DIM), (False, 2 * b_nh * 2 * HEAD_DIM)],
                2: [(True, chw), (False, 2 * chw)] * ng,
                3: [(True, d_nh * HEAD_DIM), (False, 2 * d_nh * HEAD_DIM)]}
    w_qkv = {0: a_w_qkv, 1: b_w_qkv, 2: c_w_qkv, 3: d_w_qkv}
    w_o = {0: a_w_o, 1: b_w_o, 2: c_w_o, 3: d_w_o}
    ffn_gain = norm_ffn.astype(F32)[:, :, None]
    w_gate = (ffn_w_gate * ffn_gain).astype(BF16)
    w_up = (ffn_w_up * ffn_gain).astype(BF16)
    w_down = ffn_w_down.astype(BF16)

    cos, sin = rope_tables(seqs)
    slopes_b = 2.0 ** (-8.0 * jnp.arange(1, b_nh + 1, dtype=F32) / b_nh)
    na_bias = na_bias_tables(d_rpb)

    xb, rstd = prep_norm(x)
    for i in range(depth):
        kind = i % 4
        wq = _fold_qkv(w_qkv[kind], norm_mix[i], qkv_cols[kind], qscale)
        if kind == 0:
            qkv = matmul_scaled(xb, wq, rstd, F32)
            q, k, v1 = rope_qk(qkv, cos, sin, a_q_gain, a_k_gain, a_nq, a_nk)
            o = gqa_attention(q, k, v1, seqs)
        elif kind == 1:
            qkv = matmul_scaled(xb, wq, rstd, BF16)
            lambda_init = 0.8 - 0.6 * math.exp(-0.3 * i)
            o = diff_attention(qkv, slopes_b, b_lambda_q1, b_lambda_k1, b_lambda_q2, b_lambda_k2, b_subln_gain, seqs,
                               lambda_init)
        elif kind == 2:
            outs, lses = zip(*[dilated_group(matmul_scaled(xb, wq[:, g * 3 * chw:(g + 1) * 3 * chw], rstd, BF16),
                                             c_nh, win, dil, seqs) for g, (win, dil) in enumerate(C_CONFIGS)])
            o = merge_groups(outs, lses, c_nh)
        else:
            qkv = matmul_scaled(xb, wq, rstd, BF16)
            o = neighbourhood_attention(qkv, na_bias, d_nh, seqs)
        x, xb, rstd = matmul_res(o, w_o[kind].astype(BF16), x)
        a = gateup(xb, w_gate[i], w_up[i], rstd)
        x, xb, rstd = matmul_res(a, w_down[i], x, bm_pref=512)

    y_sample = rmsnorm(x, norm_final, F32, row0=0, rows=b2 * s2).reshape(b2, s2, d)
    y_prompt = rmsnorm(x, norm_final, F32, row0=b2 * s2, rows=b1 * s1).reshape(b1, s1, d)
    return y_prompt, y_sample
```

```python
import functools
import math

import jax
import jax.numpy as jnp
import numpy as np
from jax import lax
from jax.experimental import pallas as pl
from jax.experimental.pallas import tpu as pltpu

HEAD_DIM = 128
GRID_W = 64
Q_BLOCK = 128
RMS_EPS = 1e-6
MASK_VALUE = -1e30
ROPE_THETA = 10000.0
C_CONFIGS = ((128, 1), (512, 4), (2048, 16))
NA_ROWS = 8
NA_COLS = 16
NA_BLOCK_ROWS = 4
NA_WIN_ROWS = NA_BLOCK_ROWS + NA_ROWS
NA_HEADS_PER_STEP = 8
LOG2E = math.log2(math.e)
LANES = 128
V7X_VMEM_LIMIT_BYTES = 56 * 1024 * 1024

F32 = jnp.float32
BF16 = jnp.bfloat16
NT_DIMS = (((1,), (1,)), ((), ()))


def _params(*sem):
    return pltpu.CompilerParams(dimension_semantics=sem, vmem_limit_bytes=V7X_VMEM_LIMIT_BYTES)


def _pick(n, pref, align=LANES):
    if n <= pref:
        return n
    b = (pref // align) * align
    while b >= align:
        if n % b == 0:
            return b
        b -= align
    raise ValueError(f"no block for {n} under {pref}")


def _rmsnorm_body(x_ref, g_ref, o_ref):
    x = x_ref[...]
    ms = jnp.mean(x * x, axis=-1, keepdims=True)
    o_ref[...] = (x * lax.rsqrt(ms + RMS_EPS) * g_ref[...]).astype(o_ref.dtype)


def rmsnorm(x, gain, out_dtype, row0=0, rows=None):
    t, d = x.shape
    rows = t if rows is None else rows
    bm = _pick(math.gcd(rows, row0) if row0 else rows, 256, 8)
    off = row0 // bm
    return pl.pallas_call(
        _rmsnorm_body,
        grid=(rows // bm,),
        in_specs=[pl.BlockSpec((bm, d), lambda i: (i + off, 0)),
                  pl.BlockSpec((1, d), lambda i: (0, 0))],
        out_specs=pl.BlockSpec((bm, d), lambda i: (i, 0)),
        out_shape=jax.ShapeDtypeStruct((rows, d), out_dtype),
        compiler_params=_params("parallel"),
        name="rmsnorm",
    )(x, gain.reshape(1, d).astype(F32))


def _row_scale(acc, rstd):
    return jnp.concatenate([acc[:, c:c + LANES] * rstd for c in range(0, acc.shape[1], LANES)], axis=1)


def _lane_partial_sumsq(x):
    return sum(x[:, c:c + LANES] * x[:, c:c + LANES] for c in range(0, x.shape[1], LANES))


def _finish_rstd(ssq, d_model):
    total = jnp.sum(ssq, axis=-1, keepdims=True)
    return jnp.broadcast_to(lax.rsqrt(total / d_model + RMS_EPS), ssq.shape)


def _prep_body(xs_ref, xp_ref, x_ref, xb_ref, rstd_ref, *, n_first):
    x = jnp.where(pl.program_id(0) < n_first, xs_ref[...], xp_ref[...])
    x_ref[...] = x
    xb_ref[...] = x.astype(xb_ref.dtype)
    rstd_ref[...] = _finish_rstd(_lane_partial_sumsq(x), x.shape[1])


def prep_stream(x_first, x_second):
    (t1, d), (t2, _) = x_first.shape, x_second.shape
    bm = _pick(math.gcd(t1, t2), 256, 8)
    n1, n2 = t1 // bm, t2 // bm
    row = lambda i: (i, 0)
    return pl.pallas_call(
        functools.partial(_prep_body, n_first=n1),
        grid=(n1 + n2,),
        in_specs=[pl.BlockSpec((bm, d), lambda i: (jnp.minimum(i, n1 - 1), 0)),
                  pl.BlockSpec((bm, d), lambda i: (jnp.maximum(i - n1, 0), 0))],
        out_specs=[pl.BlockSpec((bm, d), row), pl.BlockSpec((bm, d), row), pl.BlockSpec((bm, LANES), row)],
        out_shape=[jax.ShapeDtypeStruct((t1 + t2, d), F32), jax.ShapeDtypeStruct((t1 + t2, d), BF16),
                   jax.ShapeDtypeStruct((t1 + t2, LANES), F32)],
        compiler_params=_params("arbitrary"),
        name="prep_stream",
    )(x_first, x_second)


def _mm_scaled_body(a_ref, b_ref, s_ref, o_ref):
    acc = jnp.dot(a_ref[...], b_ref[...], preferred_element_type=F32)
    o_ref[...] = _row_scale(acc, s_ref[...]).astype(o_ref.dtype)


def matmul_scaled(a, b, rstd, out_dtype, bm_pref=1024, bn_pref=512):
    m, k = a.shape
    _, n = b.shape
    bm, bn = _pick(m, bm_pref, 8), _pick(n, bn_pref)
    return pl.pallas_call(
        _mm_scaled_body,
        grid=(m // bm, n // bn),
        in_specs=[pl.BlockSpec((bm, k), lambda i, j: (i, 0)),
                  pl.BlockSpec((k, bn), lambda i, j: (0, j)),
                  pl.BlockSpec((bm, LANES), lambda i, j: (i, 0))],
        out_specs=pl.BlockSpec((bm, bn), lambda i, j: (i, j)),
        out_shape=jax.ShapeDtypeStruct((m, n), out_dtype),
        compiler_params=_params("parallel", "parallel"),
        name="matmul",
    )(a, b, rstd)


def _emit_stream(x, first, last, d_model, o_ref, xb_ref, rstd_ref):
    o_ref[...] = x
    xb_ref[...] = x.astype(xb_ref.dtype)
    part = _lane_partial_sumsq(x)

    @pl.when(first)
    def _():
        rstd_ref[...] = part

    @pl.when(jnp.logical_not(first))
    def _():
        rstd_ref[...] += part

    @pl.when(last)
    def _():
        rstd_ref[...] = _finish_rstd(rstd_ref[...], d_model)


def _stream_out(m, n, bm, bn, idx):
    specs = [pl.BlockSpec((bm, bn), idx), pl.BlockSpec((bm, bn), idx),
             pl.BlockSpec((bm, LANES), lambda i, *_: (i, 0))]
    shapes = [jax.ShapeDtypeStruct((m, n), F32), jax.ShapeDtypeStruct((m, n), BF16),
              jax.ShapeDtypeStruct((m, LANES), F32)]
    return specs, shapes


def _mm_res_body(a_ref, b_ref, r_ref, o_ref, xb_ref, rstd_ref, *, d_model):
    j = pl.program_id(1)
    x = jnp.dot(a_ref[...], b_ref[...], preferred_element_type=F32) + r_ref[...]
    _emit_stream(x, j == 0, j == pl.num_programs(1) - 1, d_model, o_ref, xb_ref, rstd_ref)


def matmul_res(a, b, residual, bm_pref=1024, bn_pref=512):
    m, k = a.shape
    _, n = b.shape
    bm, bn = _pick(m, bm_pref, 8), _pick(n, bn_pref)
    out_specs, out_shape = _stream_out(m, n, bm, bn, lambda i, j: (i, j))
    return pl.pallas_call(
        functools.partial(_mm_res_body, d_model=n),
        grid=(m // bm, n // bn),
        in_specs=[pl.BlockSpec((bm, k), lambda i, j: (i, 0)),
                  pl.BlockSpec((k, bn), lambda i, j: (0, j)),
                  pl.BlockSpec((bm, bn), lambda i, j: (i, j))],
        out_specs=out_specs,
        out_shape=out_shape,
        compiler_params=_params("parallel", "arbitrary"),
        name="matmul_res",
    )(a, b, residual)


def _gateup_body(h_ref, wg_ref, wu_ref, s_ref, o_ref):
    h = h_ref[...]
    rstd = s_ref[...]
    g = _row_scale(jnp.dot(h, wg_ref[...], preferred_element_type=F32), rstd)
    u = _row_scale(jnp.dot(h, wu_ref[...], preferred_element_type=F32), rstd)
    o_ref[...] = (g / (1.0 + jnp.exp(-g)) * u).astype(o_ref.dtype)


def gateup(h, wg, wu, rstd, bm_pref=1024, bn_pref=512):
    m, k = h.shape
    _, n = wg.shape
    bm, bn = _pick(m, bm_pref, 8), min(bn_pref, n)
    return pl.pallas_call(
        _gateup_body,
        grid=(m // bm, pl.cdiv(n, bn)),
        in_specs=[pl.BlockSpec((bm, k), lambda i, j: (i, 0)),
                  pl.BlockSpec((k, bn), lambda i, j: (0, j)),
                  pl.BlockSpec((k, bn), lambda i, j: (0, j)),
                  pl.BlockSpec((bm, LANES), lambda i, j: (i, 0))],
        out_specs=pl.BlockSpec((bm, bn), lambda i, j: (i, j)),
        out_shape=jax.ShapeDtypeStruct((m, n), BF16),
        compiler_params=_params("parallel", "parallel"),
        name="gateup",
    )(h, wg, wu, rstd)


class Seqs:
    def __init__(self, b1, s1, b2, s2):
        assert b1 == 1 and s1 == 2 * s2, "layout assumes one prompt of twice the sample length"
        self.s1, self.s2, self.nb2 = s1, s2, b2
        self.p0 = b2 * s2
        self.t = self.p0 + s1

    def bounds(self, r0, unit=1):
        p0, s1, s2 = self.p0 // unit, self.s1 // unit, self.s2 // unit
        is_p = r0 >= p0
        return jnp.where(is_p, p0, r0 // s2 * s2), jnp.where(is_p, s1, s2)


def _swap_quarters(y):
    lane = lax.broadcasted_iota(jnp.int32, y.shape, 1)
    first = (lane % (HEAD_DIM // 2)) < (HEAD_DIM // 4)
    return jnp.where(first, pltpu.roll(y, HEAD_DIM - HEAD_DIM // 4, 1), pltpu.roll(y, HEAD_DIM // 4, 1))


def _rope_body(x_ref, cos_ref, sin_ref, qg_ref, kg_ref, q_ref, k_ref, v_ref, *, nq, nk, scale):
    cos, sin = cos_ref[...], sin_ref[...]

    def norm_rope(x, gain):
        ms = jnp.mean(x * x, axis=-1, keepdims=True)
        y = x * lax.rsqrt(ms + RMS_EPS) * gain
        return y * cos + _swap_quarters(y) * sin

    for h in range(nq):
        sl = slice(h * HEAD_DIM, (h + 1) * HEAD_DIM)
        q_ref[:, sl] = (norm_rope(x_ref[:, sl], qg_ref[...]) * scale).astype(q_ref.dtype)
    ones = jnp.ones((x_ref.shape[0], HEAD_DIM), v_ref.dtype)
    for h in range(nk):
        src = slice((nq + h) * HEAD_DIM, (nq + h + 1) * HEAD_DIM)
        k_ref[:, h * HEAD_DIM:(h + 1) * HEAD_DIM] = norm_rope(x_ref[:, src], kg_ref[...]).astype(k_ref.dtype)
        vsrc = slice((nq + nk + h) * HEAD_DIM, (nq + nk + h + 1) * HEAD_DIM)
        v_ref[:, 2 * h * HEAD_DIM:(2 * h + 1) * HEAD_DIM] = x_ref[:, vsrc].astype(v_ref.dtype)
        v_ref[:, (2 * h + 1) * HEAD_DIM:(2 * h + 2) * HEAD_DIM] = ones


def rope_qk(qkv, cos, sin, q_gain, k_gain, nq, nk, seqs):
    t, w = qkv.shape
    bm = _pick(seqs.s2, 256, 8)
    row = lambda i: (i, 0)
    pos = lambda i: (i - seqs.bounds(i * bm)[0] // bm, 0)
    fixed = lambda i: (0, 0)
    return pl.pallas_call(
        functools.partial(_rope_body, nq=nq, nk=nk, scale=LOG2E * HEAD_DIM ** -0.5),
        grid=(t // bm,),
        in_specs=[pl.BlockSpec((bm, w), row), pl.BlockSpec((bm, HEAD_DIM), pos), pl.BlockSpec((bm, HEAD_DIM), pos),
                  pl.BlockSpec((1, HEAD_DIM), fixed), pl.BlockSpec((1, HEAD_DIM), fixed)],
        out_specs=[pl.BlockSpec((bm, nq * HEAD_DIM), row), pl.BlockSpec((bm, nk * HEAD_DIM), row),
                   pl.BlockSpec((bm, 2 * nk * HEAD_DIM), row)],
        out_shape=[jax.ShapeDtypeStruct((t, nq * HEAD_DIM), BF16), jax.ShapeDtypeStruct((t, nk * HEAD_DIM), BF16),
                   jax.ShapeDtypeStruct((t, 2 * nk * HEAD_DIM), BF16)],
        compiler_params=_params("parallel"),
        name="rope_qk",
    )(qkv, cos, sin, q_gain.reshape(1, -1).astype(F32), k_gain.reshape(1, -1).astype(F32))


def rope_tables(seqs):
    half = HEAD_DIM // 2
    inv = ROPE_THETA ** (-jnp.arange(0, half, 2, dtype=F32) / half)
    t = jnp.arange(seqs.s1)
    ang_r = (t // GRID_W).astype(F32)[:, None] * inv
    ang_c = (t % GRID_W).astype(F32)[:, None] * inv
    cr, sr, cc, sc = jnp.cos(ang_r), jnp.sin(ang_r), jnp.cos(ang_c), jnp.sin(ang_c)
    return jnp.concatenate([cr, cr, cc, cc], axis=-1), jnp.concatenate([-sr, sr, -sc, sc], axis=-1)


def _kv_window_spec(seqs, bq, width, col0):
    return pl.BlockSpec((pl.Element(seqs.s1), pl.Element(width)),
                        lambda h, i: (pl.multiple_of(seqs.bounds(i * bq)[0], bq), pl.multiple_of(col0(h), LANES)))


GQA_ROW_CHUNK = 32
DIFF_ROW_CHUNK = 16


def _flash_pipeline(nblk, scores, probs, pv_scale, s_bufs, p_bufs):
    (s_e, s_o), (p_e, p_o) = s_bufs, p_bufs
    scores(0, s_e)
    scores(1, s_o)
    probs(0, s_e, p_e)

    def pair(jj, carry):
        j = 2 * jj + 1
        scores(j + 1, s_e)
        probs(j, s_o, p_o)
        pv_scale(j - 1, p_e, True)
        scores(j + 2, s_o)
        probs(j + 1, s_e, p_e)
        pv_scale(j, p_o, True)
        return carry

    lax.fori_loop(0, (nblk - 2) // 2, pair, 0)
    probs(nblk - 1, s_o, p_o)
    pv_scale(nblk - 2, p_e, True)
    pv_scale(nblk - 1, p_o, False)


def _flash_scratch(m_rows, bkv, acc_width, n_stats):
    return ([pltpu.VMEM((m_rows, bkv), F32)] * 2 + [pltpu.VMEM((m_rows, bkv), BF16)] * 2
            + [pltpu.VMEM((m_rows, acc_width), F32)] + [pltpu.VMEM((m_rows, 1), F32)] * n_stats)


def _gqa_body(q_ref, k_ref, v_ref, o_ref, s_e, s_o, p_e, p_o, acc_ref, m_ref, alpha_ref, *, rep, bq, bkv, seqs):
    i = pl.program_id(1)
    nblk = seqs.bounds(i * bq)[1] // bkv
    q = q_ref[...]
    qs = jnp.concatenate([q[:, r * HEAD_DIM:(r + 1) * HEAD_DIM] for r in range(rep)], axis=0)
    rows = lambda j: pl.ds(pl.multiple_of(j * bkv, bkv), bkv)

    def scores(j, s_ref):
        s_ref[...] = lax.dot_general(qs, k_ref[rows(j), :], NT_DIMS, preferred_element_type=F32)

    def probs(j, s_ref, p_ref):
        for r0 in range(0, rep * bq, GQA_ROW_CHUNK):
            rs = slice(r0, r0 + GQA_ROW_CHUNK)
            s = s_ref[rs, :]
            m = m_ref[rs, :]
            m_new = jnp.maximum(m, jnp.max(s, axis=-1, keepdims=True))
            p_ref[rs, :] = jnp.exp2(s - m_new).astype(BF16)
            alpha_ref[rs, :] = jnp.exp2(m - m_new)
            m_ref[rs, :] = m_new

    def pv_scale(j, p_ref, rescale):
        acc = acc_ref[...] + jnp.dot(p_ref[...], v_ref[rows(j), :], preferred_element_type=F32)
        acc_ref[...] = acc * alpha_ref[...] if rescale else acc

    acc_ref[...] = jnp.zeros_like(acc_ref)
    m_ref[...] = jnp.full_like(m_ref, MASK_VALUE)
    _flash_pipeline(nblk, scores, probs, pv_scale, (s_e, s_o), (p_e, p_o))
    acc = acc_ref[...]
    o = acc[:, :HEAD_DIM] / acc[:, HEAD_DIM:]
    for r in range(rep):
        o_ref[:, r * HEAD_DIM:(r + 1) * HEAD_DIM] = o[r * bq:(r + 1) * bq].astype(o_ref.dtype)


def gqa_attention(q, k, v1, seqs, bq_pref=256, bkv_pref=1024):
    t, wq = q.shape
    nk = k.shape[1] // HEAD_DIM
    rep = wq // HEAD_DIM // nk
    bq = _pick(seqs.s2, bq_pref, 16)
    bkv = _pick(seqs.s2 // 2, bkv_pref)
    return pl.pallas_call(
        functools.partial(_gqa_body, rep=rep, bq=bq, bkv=bkv, seqs=seqs),
        grid=(nk, t // bq),
        in_specs=[pl.BlockSpec((bq, rep * HEAD_DIM), lambda g, i: (i, g)),
                  _kv_window_spec(seqs, bq, HEAD_DIM, lambda g: g * HEAD_DIM),
                  _kv_window_spec(seqs, bq, 2 * HEAD_DIM, lambda g: g * 2 * HEAD_DIM)],
        out_specs=pl.BlockSpec((bq, rep * HEAD_DIM), lambda g, i: (i, g)),
        out_shape=jax.ShapeDtypeStruct((t, wq), BF16),
        scratch_shapes=_flash_scratch(rep * bq, bkv, 2 * HEAD_DIM, 2),
        compiler_params=_params("parallel", "arbitrary"),
        name="gqa_attention",
    )(q, k, v1)


def _diff_body(q_ref, k_ref, v_ref, slope_ref, tab_ref, lq1_ref, lk1_ref, lq2_ref, lk2_ref, g_ref, o_ref,
               s_e, s_o, p_e, p_o, acc_ref, m_ref, alpha_ref, l_ref, *, bq, bkv, seqs, lambda_init):
    i = pl.program_id(1)
    start, slen = seqs.bounds(i * bq)
    nblk = slen // bkv
    qpos0 = i * bq - start
    q = q_ref[...]
    q0, q1 = q[:, :HEAD_DIM], q[:, HEAD_DIM:]
    slope = slope_ref[0][:, :1] * LOG2E
    rows = lambda j: pl.ds(pl.multiple_of(j * bkv, bkv), bkv)

    def scores(j, s_ref):
        k = k_ref[rows(j), :]
        s_ref[:bq] = lax.dot_general(q0, k[:, :HEAD_DIM], NT_DIMS, preferred_element_type=F32)
        s_ref[bq:] = lax.dot_general(q1, k[:, HEAD_DIM:], NT_DIMS, preferred_element_type=F32)

    def probs(j, s_ref, p_ref):
        subs = []
        for c0 in range(0, bkv, bq):
            lead = qpos0 - (j * bkv + c0)
            kind = jnp.where(lead == 0, 2, jnp.where(lead > 0, 0, 1))
            subs.append((c0, kind, slope * jnp.abs(lead).astype(F32)))
        for r0 in range(0, 2 * bq, DIFF_ROW_CHUNK):
            rs = slice(r0, r0 + DIFF_ROW_CHUNK)
            rq = slice(r0 % bq, r0 % bq + DIFF_ROW_CHUNK)
            ss = [s_ref[rs, c0:c0 + bq] - tab_ref[0, kind, rq, :] for c0, kind, _ in subs]
            mx = functools.reduce(jnp.maximum, [jnp.max(s, axis=-1, keepdims=True) - shift
                                                for s, (_, _, shift) in zip(ss, subs)])
            m = m_ref[rs, :]
            m_new = jnp.maximum(m, mx)
            alpha = jnp.exp2(m - m_new)
            lsum = alpha * l_ref[rs, :]
            for s, (c0, _, shift) in zip(ss, subs):
                p = jnp.exp2(s - (m_new + shift))
                p_ref[rs, c0:c0 + bq] = p.astype(BF16)
                lsum = lsum + sum(p[:, c:c + LANES] for c in range(0, bq, LANES))
            l_ref[rs, :] = lsum
            alpha_ref[rs, :] = alpha
            m_ref[rs, :] = m_new

    def pv_scale(j, p_ref, rescale):
        acc = acc_ref[...] + jnp.dot(p_ref[...], v_ref[rows(j), :], preferred_element_type=F32)
        acc_ref[...] = acc * alpha_ref[...] if rescale else acc

    acc_ref[...] = jnp.zeros_like(acc_ref)
    m_ref[...] = jnp.full_like(m_ref, MASK_VALUE)
    l_ref[...] = jnp.zeros_like(l_ref)
    _flash_pipeline(nblk, scores, probs, pv_scale, (s_e, s_o), (p_e, p_o))
    o = acc_ref[...] / jnp.sum(l_ref[...], axis=-1, keepdims=True)
    lam = (jnp.exp(jnp.sum(lq1_ref[...] * lk1_ref[...], axis=-1, keepdims=True))
           - jnp.exp(jnp.sum(lq2_ref[...] * lk2_ref[...], axis=-1, keepdims=True)) + lambda_init)
    d = o[:bq] - lam * o[bq:]
    ms = jnp.mean(d * d, axis=-1, keepdims=True)
    o_ref[...] = (d * lax.rsqrt(ms + RMS_EPS) * g_ref[...] * (1.0 - lambda_init)).astype(o_ref.dtype)


def diff_attention(qkv, slopes, lq1, lk1, lq2, lk2, subln_gain, seqs, lambda_init, bq_pref=512, bkv_pref=1024):
    t, w = qkv.shape
    hw = 2 * HEAD_DIM
    nh = w // (3 * hw)
    bq = _pick(seqs.s2 // 2, bq_pref)
    bkv = _pick(seqs.s2 // 2, bkv_pref)
    assert bkv % bq == 0
    rel = (jnp.arange(bq)[:, None] - jnp.arange(bq)[None, :]).astype(F32) * (LOG2E * slopes.astype(F32))[:, None, None]
    tabs = jnp.stack([rel, -rel, jnp.abs(rel)], axis=1)
    vec = lambda a: a.reshape(1, -1).astype(F32)
    vec_spec = lambda n: pl.BlockSpec((1, n), lambda h, i: (0, 0))
    return pl.pallas_call(
        functools.partial(_diff_body, bq=bq, bkv=bkv, seqs=seqs, lambda_init=lambda_init),
        grid=(nh, t // bq),
        in_specs=[pl.BlockSpec((bq, hw), lambda h, i: (i, h)),
                  _kv_window_spec(seqs, bq, hw, lambda h: (nh + h) * hw),
                  _kv_window_spec(seqs, bq, hw, lambda h: (2 * nh + h) * hw),
                  pl.BlockSpec((1, 1, LANES), lambda h, i: (h, 0, 0)),
                  pl.BlockSpec((1, 3, bq, bq), lambda h, i: (h, 0, 0, 0)),
                  vec_spec(HEAD_DIM), vec_spec(HEAD_DIM), vec_spec(HEAD_DIM), vec_spec(HEAD_DIM), vec_spec(hw)],
        out_specs=pl.BlockSpec((bq, hw), lambda h, i: (i, h)),
        out_shape=jax.ShapeDtypeStruct((t, nh * hw), BF16),
        scratch_shapes=_flash_scratch(2 * bq, bkv, hw, 2) + [pltpu.VMEM((2 * bq, LANES), F32)],
        compiler_params=_params("parallel", "arbitrary"),
        name="diff_attention",
    )(qkv, qkv, qkv, jnp.broadcast_to(slopes.astype(F32)[:, None, None], (nh, 1, LANES)), tabs,
      vec(lq1), vec(lk1), vec(lq2), vec(lk2), vec(subln_gain))


def _dilated_body(q_ref, kp_ref, kc_ref, kn_ref, vp_ref, vc_ref, vn_ref, o_ref, lse_ref, *, nh, dilation, radius, seqs):
    n = pl.program_id(1)
    qb = Q_BLOCK
    row0 = n * qb
    seq_start, seq_len = seqs.bounds(row0, dilation)
    kw = qb + 2 * radius
    r_i = lax.broadcasted_iota(jnp.int32, (qb, kw), 0)
    c_i = lax.broadcasted_iota(jnp.int32, (qb, kw), 1)
    jrel = c_i - radius - r_i
    kabs = row0 - radius + c_i
    valid = (jnp.abs(jrel) <= radius) & (kabs >= seq_start) & (kabs < seq_start + seq_len)
    dist = (dilation * jnp.abs(jrel)).astype(F32)
    lane = lax.broadcasted_iota(jnp.int32, (qb, LANES), 1)
    lse_tile = jnp.zeros((qb, LANES), F32)
    for h in range(nh):
        sl = slice(h * HEAD_DIM, (h + 1) * HEAD_DIM)
        slope = LOG2E * 2.0 ** (-8.0 * (h + 1) / nh)
        k = jnp.concatenate([kp_ref[qb - radius:, sl], kc_ref[:, sl], kn_ref[:radius, sl]], axis=0)
        v = jnp.concatenate([vp_ref[qb - radius:, sl], vc_ref[:, sl], vn_ref[:radius, sl]], axis=0)
        s = lax.dot_general(q_ref[:, sl], k, NT_DIMS, preferred_element_type=F32)
        s = jnp.where(valid, s - slope * dist, MASK_VALUE)
        m = jnp.max(s, axis=-1, keepdims=True)
        p = jnp.exp2(s - m)
        l = jnp.sum(p, axis=-1, keepdims=True)
        o_ref[:, sl] = jnp.dot(p.astype(BF16), v, preferred_element_type=F32) / l
        lse_tile = jnp.where(lane == h, m + jnp.log2(l), lse_tile)
    lse_ref[...] = lse_tile


def dilated_group(qkv, nh, window, dilation, seqs):
    t, w = qkv.shape
    hw = nh * HEAD_DIM
    radius = window // (2 * dilation)
    assert radius <= Q_BLOCK and seqs.s2 % (dilation * Q_BLOCK) == 0
    rows = t // dilation
    nblk = rows // Q_BLOCK
    view = qkv.reshape(rows, dilation * w)

    def spec(which, shift):
        return pl.BlockSpec((Q_BLOCK, hw), lambda c, n: (jnp.clip(n + shift, 0, nblk - 1), c * 3 + which))

    o, lse = pl.pallas_call(
        functools.partial(_dilated_body, nh=nh, dilation=dilation, radius=radius, seqs=seqs),
        grid=(dilation, nblk),
        in_specs=[spec(0, 0), spec(1, -1), spec(1, 0), spec(1, 1), spec(2, -1), spec(2, 0), spec(2, 1)],
        out_specs=[pl.BlockSpec((Q_BLOCK, hw), lambda c, n: (n, c)),
                   pl.BlockSpec((Q_BLOCK, LANES), lambda c, n: (n, c))],
        out_shape=[jax.ShapeDtypeStruct((rows, dilation * hw), F32),
                   jax.ShapeDtypeStruct((rows, dilation * LANES), F32)],
        compiler_params=_params("parallel", "parallel"),
        name=f"dilated_d{dilation}",
    )(view, view, view, view, view, view, view)
    return o.reshape(t, hw), lse.reshape(t, LANES)


def _merge_body(*refs, ng, nh):
    o_refs, lse_refs, out_ref = refs[:ng], refs[ng:2 * ng], refs[2 * ng]
    lses = [r[...] for r in lse_refs]
    mx = functools.reduce(jnp.maximum, lses)
    es = [jnp.exp2(x - mx) for x in lses]
    tot = functools.reduce(lambda a, b: a + b, es)
    ws = [e / tot for e in es]
    for h in range(nh):
        sl = slice(h * HEAD_DIM, (h + 1) * HEAD_DIM)
        acc = ws[0][:, h:h + 1] * o_refs[0][:, sl]
        for gi in range(1, ng):
            acc = acc + ws[gi][:, h:h + 1] * o_refs[gi][:, sl]
        out_ref[:, sl] = acc.astype(out_ref.dtype)


def merge_groups(outs, lses, nh):
    t, hw = outs[0].shape
    ng = len(outs)
    bm = _pick(t, 256, 8)
    return pl.pallas_call(
        functools.partial(_merge_body, ng=ng, nh=nh),
        grid=(t // bm,),
        in_specs=[pl.BlockSpec((bm, hw), lambda i: (i, 0))] * ng + [pl.BlockSpec((bm, LANES), lambda i: (i, 0))] * ng,
        out_specs=pl.BlockSpec((bm, hw), lambda i: (i, 0)),
        out_shape=jax.ShapeDtypeStruct((t, hw), BF16),
        compiler_params=_params("parallel"),
        name="dilated_merge",
    )(*outs, *lses)


def _na_block_maps(seqs):
    def maps(rb):
        r0 = rb * NA_BLOCK_ROWS
        start, nrows = seqs.bounds(r0, GRID_W)
        rl = r0 - start
        ws = start + jnp.clip(rl - NA_ROWS // 2, 0, nrows - NA_WIN_ROWS)
        variant = jnp.where(rl == 0, 0, jnp.where(rl == nrows - NA_BLOCK_ROWS, 2, 1))
        return ws, variant

    return maps


def _na_body(q_ref, k_ref, v_ref, b_ref, o_ref, *, nh):
    sls = [slice(h * HEAD_DIM, (h + 1) * HEAD_DIM) for h in range(nh)]
    ss = [lax.dot_general(q_ref[:, sl], k_ref[:, sl], NT_DIMS, preferred_element_type=F32) + b_ref[0, h]
          for h, sl in enumerate(sls)]
    for sl, s in zip(sls, ss):
        m = jnp.max(s, axis=-1, keepdims=True)
        p = jnp.exp2(s - m)
        l = jnp.sum(p, axis=-1, keepdims=True)
        o_ref[:, sl] = (jnp.dot(p.astype(BF16), v_ref[:, sl], preferred_element_type=F32) / l).astype(o_ref.dtype)


def na_bias_tables(rpb):
    c = jnp.arange(GRID_W)
    cs = jnp.clip(c - NA_COLS // 2, 0, GRID_W - NA_COLS)
    col_ok = (c[None, :] >= cs[:, None]) & (c[None, :] < cs[:, None] + NA_COLS)
    col_idx = jnp.clip(c[None, :] - c[:, None] + NA_COLS - 1, 0, 2 * NA_COLS - 2)
    rpb_c = jnp.where(col_ok[None, None], rpb.astype(F32)[:, :, col_idx] * LOG2E, MASK_VALUE)
    q = np.arange(NA_BLOCK_ROWS)
    half = NA_ROWS // 2
    variants = [(np.zeros_like(q), q - half),
                (q, np.zeros_like(q)),
                (np.full_like(q, NA_WIN_ROWS - NA_ROWS), q)]
    kr = np.arange(NA_WIN_ROWS)
    tabs = []
    for off, e in variants:
        rr = kr[None, :] - off[:, None]
        valid = (rr >= 0) & (rr < NA_ROWS)
        row_off = np.clip(rr + half - 1 - e[:, None], 0, 2 * NA_ROWS - 2)
        tab = jnp.where(jnp.asarray(valid)[None, :, :, None, None], rpb_c[:, row_off], MASK_VALUE)
        tabs.append(jnp.transpose(tab, (0, 1, 3, 2, 4)).reshape(rpb.shape[0], NA_BLOCK_ROWS * GRID_W,
                                                               NA_WIN_ROWS * GRID_W))
    return jnp.stack(tabs)


def neighbourhood_attention(qkv, bias, nh, seqs):
    t, w = qkv.shape
    hw = nh * HEAD_DIM
    hps = min(NA_HEADS_PER_STEP, nh)
    gw = hps * HEAD_DIM
    assert seqs.s2 % (NA_BLOCK_ROWS * GRID_W) == 0 and seqs.s2 >= NA_WIN_ROWS * GRID_W
    maps = _na_block_maps(seqs)
    bq = NA_BLOCK_ROWS * GRID_W
    kw = NA_WIN_ROWS * GRID_W

    def win_spec(col0):
        return pl.BlockSpec((pl.Element(kw), pl.Element(gw)),
                            lambda g, rb: (pl.multiple_of(maps(rb)[0] * GRID_W, GRID_W), pl.multiple_of(col0 + g * gw, LANES)))

    return pl.pallas_call(
        functools.partial(_na_body, nh=hps),
        grid=(nh // hps, t // bq),
        in_specs=[pl.BlockSpec((bq, gw), lambda g, rb: (rb, g)),
                  win_spec(hw), win_spec(2 * hw),
                  pl.BlockSpec((1, hps, bq, kw), lambda g, rb: (maps(rb)[1], g, 0, 0))],
        out_specs=pl.BlockSpec((bq, gw), lambda g, rb: (rb, g)),
        out_shape=jax.ShapeDtypeStruct((t, hw), BF16),
        compiler_params=_params("parallel", "arbitrary"),
        name="neighbourhood_attention",
    )(qkv, qkv, qkv, bias)


def _fold_qkv(w, gain, widths, scale):
    col_scale = np.concatenate([np.full((wd,), scale if is_q else 1.0, np.float32) for is_q, wd in widths])
    return (w * (gain.astype(F32)[:, None] * col_scale[None, :])).astype(BF16)


def kernel(x_prompt, x_sample, norm_mix, norm_ffn, norm_final, a_w_qkv, a_q_gain, a_k_gain, a_w_o, b_w_qkv,
           b_lambda_q1, b_lambda_k1, b_lambda_q2, b_lambda_k2, b_subln_gain, b_w_o, c_w_qkv, c_w_o, d_w_qkv, d_rpb,
           d_w_o, ffn_w_gate, ffn_w_up, ffn_w_down):
    b1, s1, d = x_prompt.shape
    b2, s2, _ = x_sample.shape
    seqs = Seqs(b1, s1, b2, s2)
    depth = norm_mix.shape[0]

    a_nk = (a_w_qkv.shape[1] - d) // (2 * HEAD_DIM)
    a_nq = d // HEAD_DIM
    b_nh = b_w_o.shape[0] // (2 * HEAD_DIM)
    c_nh = c_w_o.shape[0] // HEAD_DIM
    d_nh = d_w_o.shape[0] // HEAD_DIM
    ng = len(C_CONFIGS)
    chw = c_nh * HEAD_DIM
    qscale = LOG2E * HEAD_DIM ** -0.5
    qkv_cols = {0: [(False, a_w_qkv.shape[1])],
                1: [(True, b_nh * 2 * HEAD_DIM), (False, 2 * b_nh * 2 * HEAD_DIM)],
                2: [(True, chw), (False, 2 * chw)] * ng,
                3: [(True, d_nh * HEAD_DIM), (False, 2 * d_nh * HEAD_DIM)]}
    w_qkv = {0: a_w_qkv, 1: b_w_qkv, 2: c_w_qkv, 3: d_w_qkv}
    w_o = {0: a_w_o, 1: b_w_o, 2: c_w_o, 3: d_w_o}

    cos, sin = rope_tables(seqs)
    slopes_b = 2.0 ** (-8.0 * jnp.arange(1, b_nh + 1, dtype=F32) / b_nh)
    na_bias = na_bias_tables(d_rpb)

    x, xb, rstd = prep_stream(x_sample.reshape(b2 * s2, d), x_prompt.reshape(b1 * s1, d))
    for i in range(depth):
        kind = i % 4
        fold = lambda w, cols: _fold_qkv(w, norm_mix[i], cols, qscale)
        wq = fold(w_qkv[kind], qkv_cols[kind]) if kind != 2 else None
        if kind == 0:
            qkv = matmul_scaled(xb, wq, rstd, F32)
            q, k, v1 = rope_qk(qkv, cos, sin, a_q_gain, a_k_gain, a_nq, a_nk, seqs)
            o = gqa_attention(q, k, v1, seqs)
        elif kind == 1:
            qkv = matmul_scaled(xb, wq, rstd, BF16)
            lambda_init = 0.8 - 0.6 * math.exp(-0.3 * i)
            o = diff_attention(qkv, slopes_b, b_lambda_q1, b_lambda_k1, b_lambda_q2, b_lambda_k2, b_subln_gain, seqs,
                               lambda_init)
        elif kind == 2:
            group_w = lambda g: fold(c_w_qkv[:, g * 3 * chw:(g + 1) * 3 * chw], qkv_cols[kind][:2])
            outs, lses = zip(*[dilated_group(matmul_scaled(xb, group_w(g), rstd, BF16), c_nh, win, dil, seqs)
                               for g, (win, dil) in enumerate(C_CONFIGS)])
            o = merge_groups(outs, lses, c_nh)
        else:
            qkv = matmul_scaled(xb, wq, rstd, BF16)
            o = neighbourhood_attention(qkv, na_bias, d_nh, seqs)
        x, xb, rstd = matmul_res(o, w_o[kind].astype(BF16), x)
        gain = norm_ffn[i].astype(F32)[:, None]
        a = gateup(xb, (ffn_w_gate[i] * gain).astype(BF16), (ffn_w_up[i] * gain).astype(BF16), rstd)
        x, xb, rstd = matmul_res(a, ffn_w_down[i].astype(BF16), x, bm_pref=512)

    y_sample = rmsnorm(x, norm_final, F32, row0=0, rows=b2 * s2).reshape(b2, s2, d)
    y_prompt = rmsnorm(x, norm_final, F32, row0=b2 * s2, rows=b1 * s1).reshape(b1, s1, d)
    return y_prompt, y_sample
```

```python
import functools
import math

import jax
import jax.numpy as jnp
import numpy as np
from jax import lax
from jax.experimental import pallas as pl
from jax.experimental.pallas import tpu as pltpu

HEAD_DIM = 128
GRID_W = 64
C_Q_BLOCK = 256
C_HALO_BLOCK = 128
RMS_EPS = 1e-6
MASK_VALUE = -1e30
ROPE_THETA = 10000.0
C_CONFIGS = ((128, 1), (512, 4), (2048, 16))
NA_ROWS = 8
NA_COLS = 16
NA_BLOCK_ROWS = 4
NA_WIN_ROWS = NA_BLOCK_ROWS + NA_ROWS
NA_HEADS_PER_STEP = 8
LOG2E = math.log2(math.e)
LANES = 128
V7X_VMEM_LIMIT_BYTES = 56 * 1024 * 1024

F32 = jnp.float32
BF16 = jnp.bfloat16
NT_DIMS = (((1,), (1,)), ((), ()))


def _params(*sem):
    return pltpu.CompilerParams(dimension_semantics=sem, vmem_limit_bytes=V7X_VMEM_LIMIT_BYTES)


def _pick(n, pref, align=LANES):
    if n <= pref:
        return n
    b = (pref // align) * align
    while b >= align:
        if n % b == 0:
            return b
        b -= align
    raise ValueError(f"no block for {n} under {pref}")


def _rmsnorm_body(x_ref, g_ref, o_ref):
    x = x_ref[...]
    ms = jnp.mean(x * x, axis=-1, keepdims=True)
    o_ref[...] = (x * lax.rsqrt(ms + RMS_EPS) * g_ref[...]).astype(o_ref.dtype)


def rmsnorm(x, gain, out_dtype, row0=0, rows=None):
    t, d = x.shape
    rows = t if rows is None else rows
    bm = _pick(math.gcd(rows, row0) if row0 else rows, 256, 8)
    off = row0 // bm
    return pl.pallas_call(
        _rmsnorm_body,
        grid=(rows // bm,),
        in_specs=[pl.BlockSpec((bm, d), lambda i: (i + off, 0)),
                  pl.BlockSpec((1, d), lambda i: (0, 0))],
        out_specs=pl.BlockSpec((bm, d), lambda i: (i, 0)),
        out_shape=jax.ShapeDtypeStruct((rows, d), out_dtype),
        compiler_params=_params("parallel"),
        name="rmsnorm",
    )(x, gain.reshape(1, d).astype(F32))


def _row_scale(acc, rstd):
    return jnp.concatenate([acc[:, c:c + LANES] * rstd for c in range(0, acc.shape[1], LANES)], axis=1)


def _lane_partial_sumsq(x):
    return sum(x[:, c:c + LANES] * x[:, c:c + LANES] for c in range(0, x.shape[1], LANES))


def _finish_rstd(ssq, d_model):
    total = jnp.sum(ssq, axis=-1, keepdims=True)
    return jnp.broadcast_to(lax.rsqrt(total / d_model + RMS_EPS), ssq.shape)


def _prep_body(xs_ref, xp_ref, x_ref, xb_ref, rstd_ref, *, n_first):
    x = jnp.where(pl.program_id(0) < n_first, xs_ref[...], xp_ref[...])
    x_ref[...] = x
    xb_ref[...] = x.astype(xb_ref.dtype)
    rstd_ref[...] = _finish_rstd(_lane_partial_sumsq(x), x.shape[1])


def prep_stream(x_first, x_second):
    (t1, d), (t2, _) = x_first.shape, x_second.shape
    bm = _pick(math.gcd(t1, t2), 256, 8)
    n1, n2 = t1 // bm, t2 // bm
    row = lambda i: (i, 0)
    return pl.pallas_call(
        functools.partial(_prep_body, n_first=n1),
        grid=(n1 + n2,),
        in_specs=[pl.BlockSpec((bm, d), lambda i: (jnp.minimum(i, n1 - 1), 0)),
                  pl.BlockSpec((bm, d), lambda i: (jnp.maximum(i - n1, 0), 0))],
        out_specs=[pl.BlockSpec((bm, d), row), pl.BlockSpec((bm, d), row), pl.BlockSpec((bm, LANES), row)],
        out_shape=[jax.ShapeDtypeStruct((t1 + t2, d), F32), jax.ShapeDtypeStruct((t1 + t2, d), BF16),
                   jax.ShapeDtypeStruct((t1 + t2, LANES), F32)],
        compiler_params=_params("arbitrary"),
        name="prep_stream",
    )(x_first, x_second)


def _mm_scaled_body(a_ref, b_ref, s_ref, o_ref):
    acc = jnp.dot(a_ref[...], b_ref[...], preferred_element_type=F32)
    o_ref[...] = _row_scale(acc, s_ref[...]).astype(o_ref.dtype)


def matmul_scaled(a, b, rstd, out_dtype, bm_pref=1024, bn_pref=512):
    m, k = a.shape
    _, n = b.shape
    bm, bn = _pick(m, bm_pref, 8), _pick(n, bn_pref)
    return pl.pallas_call(
        _mm_scaled_body,
        grid=(m // bm, n // bn),
        in_specs=[pl.BlockSpec((bm, k), lambda i, j: (i, 0)),
                  pl.BlockSpec((k, bn), lambda i, j: (0, j)),
                  pl.BlockSpec((bm, LANES), lambda i, j: (i, 0))],
        out_specs=pl.BlockSpec((bm, bn), lambda i, j: (i, j)),
        out_shape=jax.ShapeDtypeStruct((m, n), out_dtype),
        compiler_params=_params("parallel", "parallel"),
        name="matmul",
    )(a, b, rstd)


def _emit_stream(x, first, last, d_model, o_ref, xb_ref, rstd_ref):
    o_ref[...] = x
    xb_ref[...] = x.astype(xb_ref.dtype)
    part = _lane_partial_sumsq(x)

    @pl.when(first)
    def _():
        rstd_ref[...] = part

    @pl.when(jnp.logical_not(first))
    def _():
        rstd_ref[...] += part

    @pl.when(last)
    def _():
        rstd_ref[...] = _finish_rstd(rstd_ref[...], d_model)


def _stream_out(m, n, bm, bn, idx):
    specs = [pl.BlockSpec((bm, bn), idx), pl.BlockSpec((bm, bn), idx),
             pl.BlockSpec((bm, LANES), lambda i, *_: (i, 0))]
    shapes = [jax.ShapeDtypeStruct((m, n), F32), jax.ShapeDtypeStruct((m, n), BF16),
              jax.ShapeDtypeStruct((m, LANES), F32)]
    return specs, shapes


def _mm_res_body(a_ref, b_ref, r_ref, o_ref, xb_ref, rstd_ref, *, d_model):
    j = pl.program_id(1)
    x = jnp.dot(a_ref[...], b_ref[...], preferred_element_type=F32) + r_ref[...]
    _emit_stream(x, j == 0, j == pl.num_programs(1) - 1, d_model, o_ref, xb_ref, rstd_ref)


def matmul_res(a, b, residual, bm_pref=1024, bn_pref=512):
    m, k = a.shape
    _, n = b.shape
    bm, bn = _pick(m, bm_pref, 8), _pick(n, bn_pref)
    out_specs, out_shape = _stream_out(m, n, bm, bn, lambda i, j: (i, j))
    return pl.pallas_call(
        functools.partial(_mm_res_body, d_model=n),
        grid=(m // bm, n // bn),
        in_specs=[pl.BlockSpec((bm, k), lambda i, j: (i, 0)),
                  pl.BlockSpec((k, bn), lambda i, j: (0, j)),
                  pl.BlockSpec((bm, bn), lambda i, j: (i, j))],
        out_specs=out_specs,
        out_shape=out_shape,
        compiler_params=_params("parallel", "arbitrary"),
        name="matmul_res",
    )(a, b, residual)


def _gateup_body(h_ref, wg_ref, wu_ref, s_ref, o_ref):
    h = h_ref[...]
    rstd = s_ref[...]
    g = _row_scale(jnp.dot(h, wg_ref[...], preferred_element_type=F32), rstd)
    u = _row_scale(jnp.dot(h, wu_ref[...], preferred_element_type=F32), rstd)
    o_ref[...] = (g / (1.0 + jnp.exp(-g)) * u).astype(o_ref.dtype)


def gateup(h, wg, wu, rstd, bm_pref=1024, bn_pref=512):
    m, k = h.shape
    _, n = wg.shape
    bm, bn = _pick(m, bm_pref, 8), min(bn_pref, n)
    return pl.pallas_call(
        _gateup_body,
        grid=(m // bm, pl.cdiv(n, bn)),
        in_specs=[pl.BlockSpec((bm, k), lambda i, j: (i, 0)),
                  pl.BlockSpec((k, bn), lambda i, j: (0, j)),
                  pl.BlockSpec((k, bn), lambda i, j: (0, j)),
                  pl.BlockSpec((bm, LANES), lambda i, j: (i, 0))],
        out_specs=pl.BlockSpec((bm, bn), lambda i, j: (i, j)),
        out_shape=jax.ShapeDtypeStruct((m, n), BF16),
        compiler_params=_params("parallel", "parallel"),
        name="gateup",
    )(h, wg, wu, rstd)


class Seqs:
    def __init__(self, b1, s1, b2, s2):
        assert b1 == 1 and s1 == 2 * s2, "layout assumes one prompt of twice the sample length"
        self.s1, self.s2, self.nb2 = s1, s2, b2
        self.p0 = b2 * s2
        self.t = self.p0 + s1

    def bounds(self, r0, unit=1):
        p0, s1, s2 = self.p0 // unit, self.s1 // unit, self.s2 // unit
        is_p = r0 >= p0
        return jnp.where(is_p, p0, r0 // s2 * s2), jnp.where(is_p, s1, s2)


def _swap_quarters(y):
    lane = lax.broadcasted_iota(jnp.int32, y.shape, 1)
    first = (lane % (HEAD_DIM // 2)) < (HEAD_DIM // 4)
    return jnp.where(first, pltpu.roll(y, HEAD_DIM - HEAD_DIM // 4, 1), pltpu.roll(y, HEAD_DIM // 4, 1))


def _rope_body(x_ref, cos_ref, sin_ref, qg_ref, kg_ref, q_ref, k_ref, v_ref, *, nq, nk, scale):
    cos, sin = cos_ref[...], sin_ref[...]

    def norm_rope(x, gain):
        ms = jnp.mean(x * x, axis=-1, keepdims=True)
        y = x * lax.rsqrt(ms + RMS_EPS) * gain
        return y * cos + _swap_quarters(y) * sin

    for h in range(nq):
        sl = slice(h * HEAD_DIM, (h + 1) * HEAD_DIM)
        q_ref[:, sl] = (norm_rope(x_ref[:, sl], qg_ref[...]) * scale).astype(q_ref.dtype)
    ones = jnp.ones((x_ref.shape[0], HEAD_DIM), v_ref.dtype)
    for h in range(nk):
        src = slice((nq + h) * HEAD_DIM, (nq + h + 1) * HEAD_DIM)
        k_ref[:, h * HEAD_DIM:(h + 1) * HEAD_DIM] = norm_rope(x_ref[:, src], kg_ref[...]).astype(k_ref.dtype)
        vsrc = slice((nq + nk + h) * HEAD_DIM, (nq + nk + h + 1) * HEAD_DIM)
        v_ref[:, 2 * h * HEAD_DIM:(2 * h + 1) * HEAD_DIM] = x_ref[:, vsrc].astype(v_ref.dtype)
        v_ref[:, (2 * h + 1) * HEAD_DIM:(2 * h + 2) * HEAD_DIM] = ones


def rope_qk(qkv, cos, sin, q_gain, k_gain, nq, nk, seqs):
    t, w = qkv.shape
    bm = _pick(seqs.s2, 256, 8)
    row = lambda i: (i, 0)
    pos = lambda i: (i - seqs.bounds(i * bm)[0] // bm, 0)
    fixed = lambda i: (0, 0)
    return pl.pallas_call(
        functools.partial(_rope_body, nq=nq, nk=nk, scale=LOG2E * HEAD_DIM ** -0.5),
        grid=(t // bm,),
        in_specs=[pl.BlockSpec((bm, w), row), pl.BlockSpec((bm, HEAD_DIM), pos), pl.BlockSpec((bm, HEAD_DIM), pos),
                  pl.BlockSpec((1, HEAD_DIM), fixed), pl.BlockSpec((1, HEAD_DIM), fixed)],
        out_specs=[pl.BlockSpec((bm, nq * HEAD_DIM), row), pl.BlockSpec((bm, nk * HEAD_DIM), row),
                   pl.BlockSpec((bm, 2 * nk * HEAD_DIM), row)],
        out_shape=[jax.ShapeDtypeStruct((t, nq * HEAD_DIM), BF16), jax.ShapeDtypeStruct((t, nk * HEAD_DIM), BF16),
                   jax.ShapeDtypeStruct((t, 2 * nk * HEAD_DIM), BF16)],
        compiler_params=_params("parallel"),
        name="rope_qk",
    )(qkv, cos, sin, q_gain.reshape(1, -1).astype(F32), k_gain.reshape(1, -1).astype(F32))


def rope_tables(seqs):
    half = HEAD_DIM // 2
    inv = ROPE_THETA ** (-jnp.arange(0, half, 2, dtype=F32) / half)
    t = jnp.arange(seqs.s1)
    ang_r = (t // GRID_W).astype(F32)[:, None] * inv
    ang_c = (t % GRID_W).astype(F32)[:, None] * inv
    cr, sr, cc, sc = jnp.cos(ang_r), jnp.sin(ang_r), jnp.cos(ang_c), jnp.sin(ang_c)
    return jnp.concatenate([cr, cr, cc, cc], axis=-1), jnp.concatenate([-sr, sr, -sc, sc], axis=-1)


def _kv_window_spec(seqs, bq, width, col0):
    return pl.BlockSpec((pl.Element(seqs.s1), pl.Element(width)),
                        lambda h, i: (pl.multiple_of(seqs.bounds(i * bq)[0], bq), pl.multiple_of(col0(h), LANES)))


GQA_ROW_CHUNK = 32
DIFF_ROW_CHUNK = 16


def _flash_pipeline(nblk, scores, probs, pv_scale, s_bufs, p_bufs):
    (s_e, s_o), (p_e, p_o) = s_bufs, p_bufs
    scores(0, s_e)
    scores(1, s_o)
    probs(0, s_e, p_e)

    def pair(jj, carry):
        j = 2 * jj + 1
        scores(j + 1, s_e)
        probs(j, s_o, p_o)
        pv_scale(j - 1, p_e, True)
        scores(j + 2, s_o)
        probs(j + 1, s_e, p_e)
        pv_scale(j, p_o, True)
        return carry

    lax.fori_loop(0, (nblk - 2) // 2, pair, 0)
    probs(nblk - 1, s_o, p_o)
    pv_scale(nblk - 2, p_e, True)
    pv_scale(nblk - 1, p_o, False)


def _flash_scratch(m_rows, bkv, acc_width, n_stats):
    return ([pltpu.VMEM((m_rows, bkv), F32)] * 2 + [pltpu.VMEM((m_rows, bkv), BF16)] * 2
            + [pltpu.VMEM((m_rows, acc_width), F32)] + [pltpu.VMEM((m_rows, 1), F32)] * n_stats)


def _gqa_body(q_ref, k_ref, v_ref, o_ref, s_e, s_o, p_e, p_o, acc_ref, m_ref, alpha_ref, *, rep, bq, bkv, seqs):
    i = pl.program_id(1)
    nblk = seqs.bounds(i * bq)[1] // bkv
    q = q_ref[...]
    qs = jnp.concatenate([q[:, r * HEAD_DIM:(r + 1) * HEAD_DIM] for r in range(rep)], axis=0)
    rows = lambda j: pl.ds(pl.multiple_of(j * bkv, bkv), bkv)

    def scores(j, s_ref):
        s_ref[...] = lax.dot_general(qs, k_ref[rows(j), :], NT_DIMS, preferred_element_type=F32)

    def probs(j, s_ref, p_ref):
        for r0 in range(0, rep * bq, GQA_ROW_CHUNK):
            rs = slice(r0, r0 + GQA_ROW_CHUNK)
            s = s_ref[rs, :]
            m = m_ref[rs, :]
            m_new = jnp.maximum(m, jnp.max(s, axis=-1, keepdims=True))
            p_ref[rs, :] = jnp.exp2(s - m_new).astype(BF16)
            alpha_ref[rs, :] = jnp.exp2(m - m_new)
            m_ref[rs, :] = m_new

    def pv_scale(j, p_ref, rescale):
        acc = acc_ref[...] + jnp.dot(p_ref[...], v_ref[rows(j), :], preferred_element_type=F32)
        acc_ref[...] = acc * alpha_ref[...] if rescale else acc

    acc_ref[...] = jnp.zeros_like(acc_ref)
    m_ref[...] = jnp.full_like(m_ref, MASK_VALUE)
    _flash_pipeline(nblk, scores, probs, pv_scale, (s_e, s_o), (p_e, p_o))
    acc = acc_ref[...]
    o = acc[:, :HEAD_DIM] / acc[:, HEAD_DIM:]
    for r in range(rep):
        o_ref[:, r * HEAD_DIM:(r + 1) * HEAD_DIM] = o[r * bq:(r + 1) * bq].astype(o_ref.dtype)


def gqa_attention(q, k, v1, seqs, bq_pref=256, bkv_pref=1024):
    t, wq = q.shape
    nk = k.shape[1] // HEAD_DIM
    rep = wq // HEAD_DIM // nk
    bq = _pick(seqs.s2, bq_pref, 16)
    bkv = _pick(seqs.s2 // 2, bkv_pref)
    return pl.pallas_call(
        functools.partial(_gqa_body, rep=rep, bq=bq, bkv=bkv, seqs=seqs),
        grid=(nk, t // bq),
        in_specs=[pl.BlockSpec((bq, rep * HEAD_DIM), lambda g, i: (i, g)),
                  _kv_window_spec(seqs, bq, HEAD_DIM, lambda g: g * HEAD_DIM),
                  _kv_window_spec(seqs, bq, 2 * HEAD_DIM, lambda g: g * 2 * HEAD_DIM)],
        out_specs=pl.BlockSpec((bq, rep * HEAD_DIM), lambda g, i: (i, g)),
        out_shape=jax.ShapeDtypeStruct((t, wq), BF16),
        scratch_shapes=_flash_scratch(rep * bq, bkv, 2 * HEAD_DIM, 2),
        compiler_params=_params("parallel", "arbitrary"),
        name="gqa_attention",
    )(q, k, v1)


def _diff_body(q_ref, k_ref, v_ref, slope_ref, tab_ref, lq1_ref, lk1_ref, lq2_ref, lk2_ref, g_ref, o_ref,
               s_e, s_o, p_e, p_o, acc_ref, m_ref, alpha_ref, l_ref, *, bq, bkv, seqs, lambda_init):
    i = pl.program_id(1)
    start, slen = seqs.bounds(i * bq)
    nblk = slen // bkv
    qpos0 = i * bq - start
    q = q_ref[...]
    q0, q1 = q[:, :HEAD_DIM], q[:, HEAD_DIM:]
    slope = slope_ref[0][:, :1] * LOG2E
    rows = lambda j: pl.ds(pl.multiple_of(j * bkv, bkv), bkv)

    def scores(j, s_ref):
        k = k_ref[rows(j), :]
        s_ref[:bq] = lax.dot_general(q0, k[:, :HEAD_DIM], NT_DIMS, preferred_element_type=F32)
        s_ref[bq:] = lax.dot_general(q1, k[:, HEAD_DIM:], NT_DIMS, preferred_element_type=F32)

    def probs(j, s_ref, p_ref):
        subs = []
        for c0 in range(0, bkv, bq):
            lead = qpos0 - (j * bkv + c0)
            kind = jnp.where(lead == 0, 2, jnp.where(lead > 0, 0, 1))
            subs.append((c0, kind, slope * jnp.abs(lead).astype(F32)))
        for r0 in range(0, 2 * bq, DIFF_ROW_CHUNK):
            rs = slice(r0, r0 + DIFF_ROW_CHUNK)
            rq = slice(r0 % bq, r0 % bq + DIFF_ROW_CHUNK)
            ss = [s_ref[rs, c0:c0 + bq] - tab_ref[0, kind, rq, :] for c0, kind, _ in subs]
            mx = functools.reduce(jnp.maximum, [jnp.max(s, axis=-1, keepdims=True) - shift
                                                for s, (_, _, shift) in zip(ss, subs)])
            m = m_ref[rs, :]
            m_new = jnp.maximum(m, mx)
            alpha = jnp.exp2(m - m_new)
            lsum = alpha * l_ref[rs, :]
            for s, (c0, _, shift) in zip(ss, subs):
                p = jnp.exp2(s - (m_new + shift))
                p_ref[rs, c0:c0 + bq] = p.astype(BF16)
                lsum = lsum + sum(p[:, c:c + LANES] for c in range(0, bq, LANES))
            l_ref[rs, :] = lsum
            alpha_ref[rs, :] = alpha
            m_ref[rs, :] = m_new

    def pv_scale(j, p_ref, rescale):
        acc = acc_ref[...] + jnp.dot(p_ref[...], v_ref[rows(j), :], preferred_element_type=F32)
        acc_ref[...] = acc * alpha_ref[...] if rescale else acc

    acc_ref[...] = jnp.zeros_like(acc_ref)
    m_ref[...] = jnp.full_like(m_ref, MASK_VALUE)
    l_ref[...] = jnp.zeros_like(l_ref)
    _flash_pipeline(nblk, scores, probs, pv_scale, (s_e, s_o), (p_e, p_o))
    o = acc_ref[...] / jnp.sum(l_ref[...], axis=-1, keepdims=True)
    lam = (jnp.exp(jnp.sum(lq1_ref[...] * lk1_ref[...], axis=-1, keepdims=True))
           - jnp.exp(jnp.sum(lq2_ref[...] * lk2_ref[...], axis=-1, keepdims=True)) + lambda_init)
    d = o[:bq] - lam * o[bq:]
    ms = jnp.mean(d * d, axis=-1, keepdims=True)
    o_ref[...] = (d * lax.rsqrt(ms + RMS_EPS) * g_ref[...] * (1.0 - lambda_init)).astype(o_ref.dtype)


def diff_attention(qkv, slopes, lq1, lk1, lq2, lk2, subln_gain, seqs, lambda_init, bq_pref=512, bkv_pref=1024):
    t, w = qkv.shape
    hw = 2 * HEAD_DIM
    nh = w // (3 * hw)
    bq = _pick(seqs.s2 // 2, bq_pref)
    bkv = _pick(seqs.s2 // 2, bkv_pref)
    assert bkv % bq == 0
    rel = (jnp.arange(bq)[:, None] - jnp.arange(bq)[None, :]).astype(F32) * (LOG2E * slopes.astype(F32))[:, None, None]
    tabs = jnp.stack([rel, -rel, jnp.abs(rel)], axis=1)
    vec = lambda a: a.reshape(1, -1).astype(F32)
    vec_spec = lambda n: pl.BlockSpec((1, n), lambda h, i: (0, 0))
    return pl.pallas_call(
        functools.partial(_diff_body, bq=bq, bkv=bkv, seqs=seqs, lambda_init=lambda_init),
        grid=(nh, t // bq),
        in_specs=[pl.BlockSpec((bq, hw), lambda h, i: (i, h)),
                  _kv_window_spec(seqs, bq, hw, lambda h: (nh + h) * hw),
                  _kv_window_spec(seqs, bq, hw, lambda h: (2 * nh + h) * hw),
                  pl.BlockSpec((1, 1, LANES), lambda h, i: (h, 0, 0)),
                  pl.BlockSpec((1, 3, bq, bq), lambda h, i: (h, 0, 0, 0)),
                  vec_spec(HEAD_DIM), vec_spec(HEAD_DIM), vec_spec(HEAD_DIM), vec_spec(HEAD_DIM), vec_spec(hw)],
        out_specs=pl.BlockSpec((bq, hw), lambda h, i: (i, h)),
        out_shape=jax.ShapeDtypeStruct((t, nh * hw), BF16),
        scratch_shapes=_flash_scratch(2 * bq, bkv, hw, 2) + [pltpu.VMEM((2 * bq, LANES), F32)],
        compiler_params=_params("parallel", "arbitrary"),
        name="diff_attention",
    )(qkv, qkv, qkv, jnp.broadcast_to(slopes.astype(F32)[:, None, None], (nh, 1, LANES)), tabs,
      vec(lq1), vec(lk1), vec(lq2), vec(lk2), vec(subln_gain))


def _dilated_body(q_ref, kp_ref, kc_ref, kn_ref, vp_ref, vc_ref, vn_ref, o_ref, lse_ref, *, nh, dilation, radius, seqs):
    n = pl.program_id(1)
    qb, hb = C_Q_BLOCK, C_HALO_BLOCK
    row0 = n * qb
    seq_start, seq_len = seqs.bounds(row0, dilation)
    kw = qb + 2 * radius
    r_i = lax.broadcasted_iota(jnp.int32, (qb, kw), 0)
    c_i = lax.broadcasted_iota(jnp.int32, (qb, kw), 1)
    jrel = c_i - radius - r_i
    kabs = row0 - radius + c_i
    valid = (jnp.abs(jrel) <= radius) & (kabs >= seq_start) & (kabs < seq_start + seq_len)
    dist = (dilation * jnp.abs(jrel)).astype(F32)
    lane = lax.broadcasted_iota(jnp.int32, (qb, LANES), 1)
    lse_tile = jnp.zeros((qb, LANES), F32)
    sls = [slice(h * HEAD_DIM, (h + 1) * HEAD_DIM) for h in range(nh)]
    window = lambda p_ref, c_ref, n_ref, sl: jnp.concatenate([p_ref[hb - radius:, sl], c_ref[:, sl], n_ref[:radius, sl]],
                                                             axis=0)
    ss = [lax.dot_general(q_ref[:, sl], window(kp_ref, kc_ref, kn_ref, sl), NT_DIMS, preferred_element_type=F32)
          for sl in sls]
    for h, (sl, s) in enumerate(zip(sls, ss)):
        slope = LOG2E * 2.0 ** (-8.0 * (h + 1) / nh)
        v = window(vp_ref, vc_ref, vn_ref, sl)
        s = jnp.where(valid, s - slope * dist, MASK_VALUE)
        m = jnp.max(s, axis=-1, keepdims=True)
        p = jnp.exp2(s - m)
        l = jnp.sum(p, axis=-1, keepdims=True)
        o_ref[:, sl] = jnp.dot(p.astype(BF16), v, preferred_element_type=F32) / l
        lse_tile = jnp.where(lane == h, m + jnp.log2(l), lse_tile)
    lse_ref[...] = lse_tile


def residue_major(a, dilation):
    t = a.shape[0]
    return a if dilation == 1 else a.reshape(t // dilation, dilation, -1).swapaxes(0, 1).reshape(t, -1)


def dilated_group(qkv, nh, window, dilation, seqs):
    t, w = qkv.shape
    hw = nh * HEAD_DIM
    radius = window // (2 * dilation)
    qb, hb = C_Q_BLOCK, C_HALO_BLOCK
    assert radius <= hb and qb % hb == 0 and seqs.s2 % (dilation * qb) == 0
    rows = t // dilation
    nblk, nhalo = rows // qb, rows // hb

    def centre(which):
        return pl.BlockSpec((qb, hw), lambda c, n: (c * nblk + n, which))

    def halo(which, side):
        first = lambda n: n * (qb // hb) - 1 if side == 0 else (n + 1) * (qb // hb)
        return pl.BlockSpec((hb, hw), lambda c, n: (c * nhalo + jnp.clip(first(n), 0, nhalo - 1), which))

    o, lse = pl.pallas_call(
        functools.partial(_dilated_body, nh=nh, dilation=dilation, radius=radius, seqs=seqs),
        grid=(dilation, nblk),
        in_specs=[centre(0), halo(1, 0), centre(1), halo(1, 1), halo(2, 0), centre(2), halo(2, 1)],
        out_specs=[pl.BlockSpec((qb, hw), lambda c, n: (n, c)),
                   pl.BlockSpec((qb, LANES), lambda c, n: (n, c))],
        out_shape=[jax.ShapeDtypeStruct((rows, dilation * hw), F32),
                   jax.ShapeDtypeStruct((rows, dilation * LANES), F32)],
        compiler_params=_params("parallel", "parallel"),
        name=f"dilated_d{dilation}",
    )(qkv, qkv, qkv, qkv, qkv, qkv, qkv)
    return o.reshape(t, hw), lse.reshape(t, LANES)


def _merge_body(*refs, ng, nh):
    o_refs, lse_refs, out_ref = refs[:ng], refs[ng:2 * ng], refs[2 * ng]
    lses = [r[...] for r in lse_refs]
    mx = functools.reduce(jnp.maximum, lses)
    es = [jnp.exp2(x - mx) for x in lses]
    tot = functools.reduce(lambda a, b: a + b, es)
    ws = [e / tot for e in es]
    for h in range(nh):
        sl = slice(h * HEAD_DIM, (h + 1) * HEAD_DIM)
        acc = ws[0][:, h:h + 1] * o_refs[0][:, sl]
        for gi in range(1, ng):
            acc = acc + ws[gi][:, h:h + 1] * o_refs[gi][:, sl]
        out_ref[:, sl] = acc.astype(out_ref.dtype)


def merge_groups(outs, lses, nh):
    t, hw = outs[0].shape
    ng = len(outs)
    bm = _pick(t, 256, 8)
    return pl.pallas_call(
        functools.partial(_merge_body, ng=ng, nh=nh),
        grid=(t // bm,),
        in_specs=[pl.BlockSpec((bm, hw), lambda i: (i, 0))] * ng + [pl.BlockSpec((bm, LANES), lambda i: (i, 0))] * ng,
        out_specs=pl.BlockSpec((bm, hw), lambda i: (i, 0)),
        out_shape=jax.ShapeDtypeStruct((t, hw), BF16),
        compiler_params=_params("parallel"),
        name="dilated_merge",
    )(*outs, *lses)


def _na_block_maps(seqs):
    def maps(rb):
        r0 = rb * NA_BLOCK_ROWS
        start, nrows = seqs.bounds(r0, GRID_W)
        rl = r0 - start
        ws = start + jnp.clip(rl - NA_ROWS // 2, 0, nrows - NA_WIN_ROWS)
        variant = jnp.where(rl == 0, 0, jnp.where(rl == nrows - NA_BLOCK_ROWS, 2, 1))
        return ws, variant

    return maps


def _na_body(q_ref, k_ref, v_ref, b_ref, o_ref, *, nh):
    sls = [slice(h * HEAD_DIM, (h + 1) * HEAD_DIM) for h in range(nh)]
    ss = [lax.dot_general(q_ref[:, sl], k_ref[:, sl], NT_DIMS, preferred_element_type=F32) + b_ref[0, h]
          for h, sl in enumerate(sls)]
    for sl, s in zip(sls, ss):
        m = jnp.max(s, axis=-1, keepdims=True)
        p = jnp.exp2(s - m)
        l = jnp.sum(p, axis=-1, keepdims=True)
        o_ref[:, sl] = (jnp.dot(p.astype(BF16), v_ref[:, sl], preferred_element_type=F32) / l).astype(o_ref.dtype)


def na_bias_tables(rpb):
    c = jnp.arange(GRID_W)
    cs = jnp.clip(c - NA_COLS // 2, 0, GRID_W - NA_COLS)
    col_ok = (c[None, :] >= cs[:, None]) & (c[None, :] < cs[:, None] + NA_COLS)
    col_idx = jnp.clip(c[None, :] - c[:, None] + NA_COLS - 1, 0, 2 * NA_COLS - 2)
    rpb_c = jnp.where(col_ok[None, None], rpb.astype(F32)[:, :, col_idx] * LOG2E, MASK_VALUE)
    q = np.arange(NA_BLOCK_ROWS)
    half = NA_ROWS // 2
    variants = [(np.zeros_like(q), q - half),
                (q, np.zeros_like(q)),
                (np.full_like(q, NA_WIN_ROWS - NA_ROWS), q)]
    kr = np.arange(NA_WIN_ROWS)
    tabs = []
    for off, e in variants:
        rr = kr[None, :] - off[:, None]
        valid = (rr >= 0) & (rr < NA_ROWS)
        row_off = np.clip(rr + half - 1 - e[:, None], 0, 2 * NA_ROWS - 2)
        tab = jnp.where(jnp.asarray(valid)[None, :, :, None, None], rpb_c[:, row_off], MASK_VALUE)
        tabs.append(jnp.transpose(tab, (0, 1, 3, 2, 4)).reshape(rpb.shape[0], NA_BLOCK_ROWS * GRID_W,
                                                               NA_WIN_ROWS * GRID_W))
    return jnp.stack(tabs)


def neighbourhood_attention(qkv, bias, nh, seqs):
    t, w = qkv.shape
    hw = nh * HEAD_DIM
    hps = min(NA_HEADS_PER_STEP, nh)
    gw = hps * HEAD_DIM
    assert seqs.s2 % (NA_BLOCK_ROWS * GRID_W) == 0 and seqs.s2 >= NA_WIN_ROWS * GRID_W
    maps = _na_block_maps(seqs)
    bq = NA_BLOCK_ROWS * GRID_W
    kw = NA_WIN_ROWS * GRID_W

    def win_spec(col0):
        return pl.BlockSpec((pl.Element(kw), pl.Element(gw)),
                            lambda g, rb: (pl.multiple_of(maps(rb)[0] * GRID_W, GRID_W), pl.multiple_of(col0 + g * gw, LANES)))

    return pl.pallas_call(
        functools.partial(_na_body, nh=hps),
        grid=(nh // hps, t // bq),
        in_specs=[pl.BlockSpec((bq, gw), lambda g, rb: (rb, g)),
                  win_spec(hw), win_spec(2 * hw),
                  pl.BlockSpec((1, hps, bq, kw), lambda g, rb: (maps(rb)[1], g, 0, 0))],
        out_specs=pl.BlockSpec((bq, gw), lambda g, rb: (rb, g)),
        out_shape=jax.ShapeDtypeStruct((t, hw), BF16),
        compiler_params=_params("parallel", "arbitrary"),
        name="neighbourhood_attention",
    )(qkv, qkv, qkv, bias)


def _fold_qkv(w, gain, widths, scale):
    col_scale = np.concatenate([np.full((wd,), scale if is_q else 1.0, np.float32) for is_q, wd in widths])
    return (w * (gain.astype(F32)[:, None] * col_scale[None, :])).astype(BF16)


def kernel(x_prompt, x_sample, norm_mix, norm_ffn, norm_final, a_w_qkv, a_q_gain, a_k_gain, a_w_o, b_w_qkv,
           b_lambda_q1, b_lambda_k1, b_lambda_q2, b_lambda_k2, b_subln_gain, b_w_o, c_w_qkv, c_w_o, d_w_qkv, d_rpb,
           d_w_o, ffn_w_gate, ffn_w_up, ffn_w_down):
    b1, s1, d = x_prompt.shape
    b2, s2, _ = x_sample.shape
    seqs = Seqs(b1, s1, b2, s2)
    depth = norm_mix.shape[0]

    a_nk = (a_w_qkv.shape[1] - d) // (2 * HEAD_DIM)
    a_nq = d // HEAD_DIM
    b_nh = b_w_o.shape[0] // (2 * HEAD_DIM)
    c_nh = c_w_o.shape[0] // HEAD_DIM
    d_nh = d_w_o.shape[0] // HEAD_DIM
    ng = len(C_CONFIGS)
    chw = c_nh * HEAD_DIM
    qscale = LOG2E * HEAD_DIM ** -0.5
    qkv_cols = {0: [(False, a_w_qkv.shape[1])],
                1: [(True, b_nh * 2 * HEAD_DIM), (False, 2 * b_nh * 2 * HEAD_DIM)],
                2: [(True, chw), (False, 2 * chw)] * ng,
                3: [(True, d_nh * HEAD_DIM), (False, 2 * d_nh * HEAD_DIM)]}
    w_qkv = {0: a_w_qkv, 1: b_w_qkv, 2: c_w_qkv, 3: d_w_qkv}
    w_o = {0: a_w_o, 1: b_w_o, 2: c_w_o, 3: d_w_o}

    cos, sin = rope_tables(seqs)
    slopes_b = 2.0 ** (-8.0 * jnp.arange(1, b_nh + 1, dtype=F32) / b_nh)
    na_bias = na_bias_tables(d_rpb)

    x, xb, rstd = prep_stream(x_sample.reshape(b2 * s2, d), x_prompt.reshape(b1 * s1, d))
    for i in range(depth):
        kind = i % 4
        fold = lambda w, cols: _fold_qkv(w, norm_mix[i], cols, qscale)
        wq = fold(w_qkv[kind], qkv_cols[kind]) if kind != 2 else None
        if kind == 0:
            qkv = matmul_scaled(xb, wq, rstd, F32)
            q, k, v1 = rope_qk(qkv, cos, sin, a_q_gain, a_k_gain, a_nq, a_nk, seqs)
            o = gqa_attention(q, k, v1, seqs)
        elif kind == 1:
            qkv = matmul_scaled(xb, wq, rstd, BF16)
            lambda_init = 0.8 - 0.6 * math.exp(-0.3 * i)
            o = diff_attention(qkv, slopes_b, b_lambda_q1, b_lambda_k1, b_lambda_q2, b_lambda_k2, b_subln_gain, seqs,
                               lambda_init)
        elif kind == 2:
            group_w = lambda g: fold(c_w_qkv[:, g * 3 * chw:(g + 1) * 3 * chw], qkv_cols[kind][:2])
            group_qkv = lambda g, dil: matmul_scaled(residue_major(xb, dil), group_w(g), residue_major(rstd, dil), BF16)
            outs, lses = zip(*[dilated_group(group_qkv(g, dil), c_nh, win, dil, seqs)
                               for g, (win, dil) in enumerate(C_CONFIGS)])
            o = merge_groups(outs, lses, c_nh)
        else:
            qkv = matmul_scaled(xb, wq, rstd, BF16)
            o = neighbourhood_attention(qkv, na_bias, d_nh, seqs)
        x, xb, rstd = matmul_res(o, w_o[kind].astype(BF16), x)
        gain = norm_ffn[i].astype(F32)[:, None]
        a = gateup(xb, (ffn_w_gate[i] * gain).astype(BF16), (ffn_w_up[i] * gain).astype(BF16), rstd)
        x, xb, rstd = matmul_res(a, ffn_w_down[i].astype(BF16), x, bm_pref=512)

    y_sample = rmsnorm(x, norm_final, F32, row0=0, rows=b2 * s2).reshape(b2, s2, d)
    y_prompt = rmsnorm(x, norm_final, F32, row0=b2 * s2, rows=b1 * s1).reshape(b1, s1, d)
    return y_prompt, y_sample
```

```python
import functools
import math

import jax
import jax.numpy as jnp
import numpy as np
from jax import lax
from jax.experimental import pallas as pl
from jax.experimental.pallas import tpu as pltpu

HEAD_DIM = 128
GRID_W = 64
C_Q_BLOCK = 256
C_HALO_BLOCK = 128
RMS_EPS = 1e-6
MASK_VALUE = -1e30
ROPE_THETA = 10000.0
C_CONFIGS = ((128, 1), (512, 4), (2048, 16))
NA_ROWS = 8
NA_COLS = 16
NA_BLOCK_ROWS = 4
NA_WIN_ROWS = NA_BLOCK_ROWS + NA_ROWS
NA_HEADS_PER_STEP = 8
LOG2E = math.log2(math.e)
LANES = 128
V7X_VMEM_LIMIT_BYTES = 56 * 1024 * 1024

F32 = jnp.float32
BF16 = jnp.bfloat16
NT_DIMS = (((1,), (1,)), ((), ()))


def _params(*sem):
    return pltpu.CompilerParams(dimension_semantics=sem, vmem_limit_bytes=V7X_VMEM_LIMIT_BYTES)


def _pick(n, pref, align=LANES):
    if n <= pref:
        return n
    b = (pref // align) * align
    while b >= align:
        if n % b == 0:
            return b
        b -= align
    raise ValueError(f"no block for {n} under {pref}")


def _rmsnorm_body(x_ref, g_ref, o_ref):
    x = x_ref[...]
    ms = jnp.mean(x * x, axis=-1, keepdims=True)
    o_ref[...] = (x * lax.rsqrt(ms + RMS_EPS) * g_ref[...]).astype(o_ref.dtype)


def rmsnorm(x, gain, out_dtype, row0=0, rows=None):
    t, d = x.shape
    rows = t if rows is None else rows
    bm = _pick(math.gcd(rows, row0) if row0 else rows, 256, 8)
    off = row0 // bm
    return pl.pallas_call(
        _rmsnorm_body,
        grid=(rows // bm,),
        in_specs=[pl.BlockSpec((bm, d), lambda i: (i + off, 0)),
                  pl.BlockSpec((1, d), lambda i: (0, 0))],
        out_specs=pl.BlockSpec((bm, d), lambda i: (i, 0)),
        out_shape=jax.ShapeDtypeStruct((rows, d), out_dtype),
        compiler_params=_params("parallel"),
        name="rmsnorm",
    )(x, gain.reshape(1, d).astype(F32))


def _row_scale(acc, rstd):
    return jnp.concatenate([acc[:, c:c + LANES] * rstd for c in range(0, acc.shape[1], LANES)], axis=1)


def _lane_partial_sumsq(x):
    return sum(x[:, c:c + LANES] * x[:, c:c + LANES] for c in range(0, x.shape[1], LANES))


def _finish_rstd(ssq, d_model):
    total = jnp.sum(ssq, axis=-1, keepdims=True)
    return jnp.broadcast_to(lax.rsqrt(total / d_model + RMS_EPS), ssq.shape)


def _prep_body(xs_ref, xp_ref, x_ref, xb_ref, rstd_ref, *, n_first):
    x = jnp.where(pl.program_id(0) < n_first, xs_ref[...], xp_ref[...])
    x_ref[...] = x
    xb_ref[...] = x.astype(xb_ref.dtype)
    rstd_ref[...] = _finish_rstd(_lane_partial_sumsq(x), x.shape[1])


def prep_stream(x_first, x_second):
    (t1, d), (t2, _) = x_first.shape, x_second.shape
    bm = _pick(math.gcd(t1, t2), 256, 8)
    n1, n2 = t1 // bm, t2 // bm
    row = lambda i: (i, 0)
    return pl.pallas_call(
        functools.partial(_prep_body, n_first=n1),
        grid=(n1 + n2,),
        in_specs=[pl.BlockSpec((bm, d), lambda i: (jnp.minimum(i, n1 - 1), 0)),
                  pl.BlockSpec((bm, d), lambda i: (jnp.maximum(i - n1, 0), 0))],
        out_specs=[pl.BlockSpec((bm, d), row), pl.BlockSpec((bm, d), row), pl.BlockSpec((bm, LANES), row)],
        out_shape=[jax.ShapeDtypeStruct((t1 + t2, d), F32), jax.ShapeDtypeStruct((t1 + t2, d), BF16),
                   jax.ShapeDtypeStruct((t1 + t2, LANES), F32)],
        compiler_params=_params("arbitrary"),
        name="prep_stream",
    )(x_first, x_second)


def _mm_scaled_body(a_ref, b_ref, s_ref, o_ref):
    acc = jnp.dot(a_ref[...], b_ref[...], preferred_element_type=F32)
    o_ref[...] = _row_scale(acc, s_ref[...]).astype(o_ref.dtype)


def matmul_scaled(a, b, rstd, out_dtype, bm_pref=1024, bn_pref=1024):
    m, k = a.shape
    _, n = b.shape
    bm, bn = _pick(m, bm_pref, 8), _pick(n, bn_pref)
    return pl.pallas_call(
        _mm_scaled_body,
        grid=(m // bm, n // bn),
        in_specs=[pl.BlockSpec((bm, k), lambda i, j: (i, 0)),
                  pl.BlockSpec((k, bn), lambda i, j: (0, j)),
                  pl.BlockSpec((bm, LANES), lambda i, j: (i, 0))],
        out_specs=pl.BlockSpec((bm, bn), lambda i, j: (i, j)),
        out_shape=jax.ShapeDtypeStruct((m, n), out_dtype),
        compiler_params=_params("parallel", "parallel"),
        name="matmul",
    )(a, b, rstd)


def _emit_stream(x, first, last, d_model, o_ref, xb_ref, rstd_ref):
    o_ref[...] = x
    xb_ref[...] = x.astype(xb_ref.dtype)
    part = _lane_partial_sumsq(x)

    @pl.when(first)
    def _():
        rstd_ref[...] = part

    @pl.when(jnp.logical_not(first))
    def _():
        rstd_ref[...] += part

    @pl.when(last)
    def _():
        rstd_ref[...] = _finish_rstd(rstd_ref[...], d_model)


def _stream_out(m, n, bm, bn, idx):
    specs = [pl.BlockSpec((bm, bn), idx), pl.BlockSpec((bm, bn), idx),
             pl.BlockSpec((bm, LANES), lambda i, *_: (i, 0))]
    shapes = [jax.ShapeDtypeStruct((m, n), F32), jax.ShapeDtypeStruct((m, n), BF16),
              jax.ShapeDtypeStruct((m, LANES), F32)]
    return specs, shapes


def _mm_res_body(a_ref, b_ref, r_ref, o_ref, xb_ref, rstd_ref, *, d_model):
    j = pl.program_id(1)
    x = jnp.dot(a_ref[...], b_ref[...], preferred_element_type=F32) + r_ref[...]
    _emit_stream(x, j == 0, j == pl.num_programs(1) - 1, d_model, o_ref, xb_ref, rstd_ref)


def matmul_res(a, b, residual, bm_pref=1024, bn_pref=512):
    m, k = a.shape
    _, n = b.shape
    bm, bn = _pick(m, bm_pref, 8), _pick(n, bn_pref)
    out_specs, out_shape = _stream_out(m, n, bm, bn, lambda i, j: (i, j))
    return pl.pallas_call(
        functools.partial(_mm_res_body, d_model=n),
        grid=(m // bm, n // bn),
        in_specs=[pl.BlockSpec((bm, k), lambda i, j: (i, 0)),
                  pl.BlockSpec((k, bn), lambda i, j: (0, j)),
                  pl.BlockSpec((bm, bn), lambda i, j: (i, j))],
        out_specs=out_specs,
        out_shape=out_shape,
        compiler_params=_params("parallel", "arbitrary"),
        name="matmul_res",
    )(a, b, residual)


def _gateup_body(h_ref, wg_ref, wu_ref, s_ref, o_ref):
    h = h_ref[...]
    rstd = s_ref[...]
    g = _row_scale(jnp.dot(h, wg_ref[...], preferred_element_type=F32), rstd)
    u = _row_scale(jnp.dot(h, wu_ref[...], preferred_element_type=F32), rstd)
    o_ref[...] = (g / (1.0 + jnp.exp(-g)) * u).astype(o_ref.dtype)


def gateup(h, wg, wu, rstd, bm_pref=1024, bn_pref=512):
    m, k = h.shape
    _, n = wg.shape
    bm, bn = _pick(m, bm_pref, 8), min(bn_pref, n)
    return pl.pallas_call(
        _gateup_body,
        grid=(m // bm, pl.cdiv(n, bn)),
        in_specs=[pl.BlockSpec((bm, k), lambda i, j: (i, 0)),
                  pl.BlockSpec((k, bn), lambda i, j: (0, j)),
                  pl.BlockSpec((k, bn), lambda i, j: (0, j)),
                  pl.BlockSpec((bm, LANES), lambda i, j: (i, 0))],
        out_specs=pl.BlockSpec((bm, bn), lambda i, j: (i, j)),
        out_shape=jax.ShapeDtypeStruct((m, n), BF16),
        compiler_params=_params("parallel", "parallel"),
        name="gateup",
    )(h, wg, wu, rstd)


class Seqs:
    def __init__(self, b1, s1, b2, s2):
        assert b1 == 1 and s1 == 2 * s2, "layout assumes one prompt of twice the sample length"
        self.s1, self.s2, self.nb2 = s1, s2, b2
        self.p0 = b2 * s2
        self.t = self.p0 + s1

    def bounds(self, r0, unit=1):
        p0, s1, s2 = self.p0 // unit, self.s1 // unit, self.s2 // unit
        is_p = r0 >= p0
        return jnp.where(is_p, p0, r0 // s2 * s2), jnp.where(is_p, s1, s2)


def _swap_quarters(y):
    lane = lax.broadcasted_iota(jnp.int32, y.shape, 1)
    first = (lane % (HEAD_DIM // 2)) < (HEAD_DIM // 4)
    return jnp.where(first, pltpu.roll(y, HEAD_DIM - HEAD_DIM // 4, 1), pltpu.roll(y, HEAD_DIM // 4, 1))


def _rope_body(x_ref, cos_ref, sin_ref, qg_ref, kg_ref, q_ref, k_ref, v_ref, *, nq, nk, scale):
    cos, sin = cos_ref[...], sin_ref[...]

    def norm_rope(x, gain):
        ms = jnp.mean(x * x, axis=-1, keepdims=True)
        y = x * lax.rsqrt(ms + RMS_EPS) * gain
        return y * cos + _swap_quarters(y) * sin

    for h in range(nq):
        sl = slice(h * HEAD_DIM, (h + 1) * HEAD_DIM)
        q_ref[:, sl] = (norm_rope(x_ref[:, sl], qg_ref[...]) * scale).astype(q_ref.dtype)
    ones = jnp.ones((x_ref.shape[0], HEAD_DIM), v_ref.dtype)
    for h in range(nk):
        src = slice((nq + h) * HEAD_DIM, (nq + h + 1) * HEAD_DIM)
        k_ref[:, h * HEAD_DIM:(h + 1) * HEAD_DIM] = norm_rope(x_ref[:, src], kg_ref[...]).astype(k_ref.dtype)
        vsrc = slice((nq + nk + h) * HEAD_DIM, (nq + nk + h + 1) * HEAD_DIM)
        v_ref[:, 2 * h * HEAD_DIM:(2 * h + 1) * HEAD_DIM] = x_ref[:, vsrc].astype(v_ref.dtype)
        v_ref[:, (2 * h + 1) * HEAD_DIM:(2 * h + 2) * HEAD_DIM] = ones


def rope_qk(qkv, cos, sin, q_gain, k_gain, nq, nk, seqs):
    t, w = qkv.shape
    bm = _pick(seqs.s2, 256, 8)
    row = lambda i: (i, 0)
    pos = lambda i: (i - seqs.bounds(i * bm)[0] // bm, 0)
    fixed = lambda i: (0, 0)
    return pl.pallas_call(
        functools.partial(_rope_body, nq=nq, nk=nk, scale=LOG2E * HEAD_DIM ** -0.5),
        grid=(t // bm,),
        in_specs=[pl.BlockSpec((bm, w), row), pl.BlockSpec((bm, HEAD_DIM), pos), pl.BlockSpec((bm, HEAD_DIM), pos),
                  pl.BlockSpec((1, HEAD_DIM), fixed), pl.BlockSpec((1, HEAD_DIM), fixed)],
        out_specs=[pl.BlockSpec((bm, nq * HEAD_DIM), row), pl.BlockSpec((bm, nk * HEAD_DIM), row),
                   pl.BlockSpec((bm, 2 * nk * HEAD_DIM), row)],
        out_shape=[jax.ShapeDtypeStruct((t, nq * HEAD_DIM), BF16), jax.ShapeDtypeStruct((t, nk * HEAD_DIM), BF16),
                   jax.ShapeDtypeStruct((t, 2 * nk * HEAD_DIM), BF16)],
        compiler_params=_params("parallel"),
        name="rope_qk",
    )(qkv, cos, sin, q_gain.reshape(1, -1).astype(F32), k_gain.reshape(1, -1).astype(F32))


def rope_tables(seqs):
    half = HEAD_DIM // 2
    inv = ROPE_THETA ** (-jnp.arange(0, half, 2, dtype=F32) / half)
    t = jnp.arange(seqs.s1)
    ang_r = (t // GRID_W).astype(F32)[:, None] * inv
    ang_c = (t % GRID_W).astype(F32)[:, None] * inv
    cr, sr, cc, sc = jnp.cos(ang_r), jnp.sin(ang_r), jnp.cos(ang_c), jnp.sin(ang_c)
    return jnp.concatenate([cr, cr, cc, cc], axis=-1), jnp.concatenate([-sr, sr, -sc, sc], axis=-1)


def _kv_window_spec(seqs, bq, width, col0):
    return pl.BlockSpec((pl.Element(seqs.s1), pl.Element(width)),
                        lambda h, i: (pl.multiple_of(seqs.bounds(i * bq)[0], bq), pl.multiple_of(col0(h), LANES)))


GQA_ROW_CHUNK = 32
DIFF_ROW_CHUNK = 16


def _flash_pipeline(nblk, scores, probs, pv_scale, s_bufs, p_bufs):
    (s_e, s_o), (p_e, p_o) = s_bufs, p_bufs
    scores(0, s_e)
    scores(1, s_o)
    probs(0, s_e, p_e)

    def pair(jj, carry):
        j = 2 * jj + 1
        scores(j + 1, s_e)
        probs(j, s_o, p_o)
        pv_scale(j - 1, p_e, True)
        scores(j + 2, s_o)
        probs(j + 1, s_e, p_e)
        pv_scale(j, p_o, True)
        return carry

    lax.fori_loop(0, (nblk - 2) // 2, pair, 0)
    probs(nblk - 1, s_o, p_o)
    pv_scale(nblk - 2, p_e, True)
    pv_scale(nblk - 1, p_o, False)


def _flash_scratch(m_rows, bkv, acc_width, n_stats):
    return ([pltpu.VMEM((m_rows, bkv), F32)] * 2 + [pltpu.VMEM((m_rows, bkv), BF16)] * 2
            + [pltpu.VMEM((m_rows, acc_width), F32)] + [pltpu.VMEM((m_rows, 1), F32)] * n_stats)


def _gqa_body(q_ref, k_ref, v_ref, o_ref, s_e, s_o, p_e, p_o, acc_ref, m_ref, alpha_ref, *, rep, bq, bkv, seqs):
    i = pl.program_id(1)
    nblk = seqs.bounds(i * bq)[1] // bkv
    q = q_ref[...]
    qs = jnp.concatenate([q[:, r * HEAD_DIM:(r + 1) * HEAD_DIM] for r in range(rep)], axis=0)
    rows = lambda j: pl.ds(pl.multiple_of(j * bkv, bkv), bkv)

    def scores(j, s_ref):
        s_ref[...] = lax.dot_general(qs, k_ref[rows(j), :], NT_DIMS, preferred_element_type=F32)

    def probs(j, s_ref, p_ref):
        for r0 in range(0, rep * bq, GQA_ROW_CHUNK):
            rs = slice(r0, r0 + GQA_ROW_CHUNK)
            s = s_ref[rs, :]
            m = m_ref[rs, :]
            m_new = jnp.maximum(m, jnp.max(s, axis=-1, keepdims=True))
            p_ref[rs, :] = jnp.exp2(s - m_new).astype(BF16)
            alpha_ref[rs, :] = jnp.exp2(m - m_new)
            m_ref[rs, :] = m_new

    def pv_scale(j, p_ref, rescale):
        acc = acc_ref[...] + jnp.dot(p_ref[...], v_ref[rows(j), :], preferred_element_type=F32)
        acc_ref[...] = acc * alpha_ref[...] if rescale else acc

    acc_ref[...] = jnp.zeros_like(acc_ref)
    m_ref[...] = jnp.full_like(m_ref, MASK_VALUE)
    _flash_pipeline(nblk, scores, probs, pv_scale, (s_e, s_o), (p_e, p_o))
    acc = acc_ref[...]
    o = acc[:, :HEAD_DIM] / acc[:, HEAD_DIM:]
    for r in range(rep):
        o_ref[:, r * HEAD_DIM:(r + 1) * HEAD_DIM] = o[r * bq:(r + 1) * bq].astype(o_ref.dtype)


def gqa_attention(q, k, v1, seqs, bq_pref=256, bkv_pref=1024):
    t, wq = q.shape
    nk = k.shape[1] // HEAD_DIM
    rep = wq // HEAD_DIM // nk
    bq = _pick(seqs.s2, bq_pref, 16)
    bkv = _pick(seqs.s2 // 2, bkv_pref)
    return pl.pallas_call(
        functools.partial(_gqa_body, rep=rep, bq=bq, bkv=bkv, seqs=seqs),
        grid=(nk, t // bq),
        in_specs=[pl.BlockSpec((bq, rep * HEAD_DIM), lambda g, i: (i, g)),
                  _kv_window_spec(seqs, bq, HEAD_DIM, lambda g: g * HEAD_DIM),
                  _kv_window_spec(seqs, bq, 2 * HEAD_DIM, lambda g: g * 2 * HEAD_DIM)],
        out_specs=pl.BlockSpec((bq, rep * HEAD_DIM), lambda g, i: (i, g)),
        out_shape=jax.ShapeDtypeStruct((t, wq), BF16),
        scratch_shapes=_flash_scratch(rep * bq, bkv, 2 * HEAD_DIM, 2),
        compiler_params=_params("parallel", "arbitrary"),
        name="gqa_attention",
    )(q, k, v1)


def _diff_body(q_ref, k_ref, v_ref, slope_ref, tab_ref, lq1_ref, lk1_ref, lq2_ref, lk2_ref, g_ref, o_ref,
               s_e, s_o, p_e, p_o, acc_ref, m_ref, alpha_ref, l_ref, *, bq, bkv, seqs, lambda_init):
    i = pl.program_id(1)
    start, slen = seqs.bounds(i * bq)
    nblk = slen // bkv
    qpos0 = i * bq - start
    q = q_ref[...]
    q0, q1 = q[:, :HEAD_DIM], q[:, HEAD_DIM:]
    slope = slope_ref[0][:, :1] * LOG2E
    rows = lambda j: pl.ds(pl.multiple_of(j * bkv, bkv), bkv)

    def scores(j, s_ref):
        k = k_ref[rows(j), :]
        s_ref[:bq] = lax.dot_general(q0, k[:, :HEAD_DIM], NT_DIMS, preferred_element_type=F32)
        s_ref[bq:] = lax.dot_general(q1, k[:, HEAD_DIM:], NT_DIMS, preferred_element_type=F32)

    def probs(j, s_ref, p_ref):
        subs = []
        for c0 in range(0, bkv, bq):
            lead = qpos0 - (j * bkv + c0)
            kind = jnp.where(lead == 0, 2, jnp.where(lead > 0, 0, 1))
            subs.append((c0, kind, slope * jnp.abs(lead).astype(F32)))
        for r0 in range(0, 2 * bq, DIFF_ROW_CHUNK):
            rs = slice(r0, r0 + DIFF_ROW_CHUNK)
            rq = slice(r0 % bq, r0 % bq + DIFF_ROW_CHUNK)
            ss = [s_ref[rs, c0:c0 + bq] - tab_ref[0, kind, rq, :] for c0, kind, _ in subs]
            mx = functools.reduce(jnp.maximum, [jnp.max(s, axis=-1, keepdims=True) - shift
                                                for s, (_, _, shift) in zip(ss, subs)])
            m = m_ref[rs, :]
            m_new = jnp.maximum(m, mx)
            alpha = jnp.exp2(m - m_new)
            lsum = alpha * l_ref[rs, :]
            for s, (c0, _, shift) in zip(ss, subs):
                p = jnp.exp2(s - (m_new + shift))
                p_ref[rs, c0:c0 + bq] = p.astype(BF16)
                lsum = lsum + sum(p[:, c:c + LANES] for c in range(0, bq, LANES))
            l_ref[rs, :] = lsum
            alpha_ref[rs, :] = alpha
            m_ref[rs, :] = m_new

    def pv_scale(j, p_ref, rescale):
        acc = acc_ref[...] + jnp.dot(p_ref[...], v_ref[rows(j), :], preferred_element_type=F32)
        acc_ref[...] = acc * alpha_ref[...] if rescale else acc

    acc_ref[...] = jnp.zeros_like(acc_ref)
    m_ref[...] = jnp.full_like(m_ref, MASK_VALUE)
    l_ref[...] = jnp.zeros_like(l_ref)
    _flash_pipeline(nblk, scores, probs, pv_scale, (s_e, s_o), (p_e, p_o))
    o = acc_ref[...] / jnp.sum(l_ref[...], axis=-1, keepdims=True)
    lam = (jnp.exp(jnp.sum(lq1_ref[...] * lk1_ref[...], axis=-1, keepdims=True))
           - jnp.exp(jnp.sum(lq2_ref[...] * lk2_ref[...], axis=-1, keepdims=True)) + lambda_init)
    d = o[:bq] - lam * o[bq:]
    ms = jnp.mean(d * d, axis=-1, keepdims=True)
    o_ref[...] = (d * lax.rsqrt(ms + RMS_EPS) * g_ref[...] * (1.0 - lambda_init)).astype(o_ref.dtype)


def diff_attention(qkv, slopes, lq1, lk1, lq2, lk2, subln_gain, seqs, lambda_init, bq_pref=512, bkv_pref=1024):
    t, w = qkv.shape
    hw = 2 * HEAD_DIM
    nh = w // (3 * hw)
    bq = _pick(seqs.s2 // 2, bq_pref)
    bkv = _pick(seqs.s2 // 2, bkv_pref)
    assert bkv % bq == 0
    rel = (jnp.arange(bq)[:, None] - jnp.arange(bq)[None, :]).astype(F32) * (LOG2E * slopes.astype(F32))[:, None, None]
    tabs = jnp.stack([rel, -rel, jnp.abs(rel)], axis=1)
    vec = lambda a: a.reshape(1, -1).astype(F32)
    vec_spec = lambda n: pl.BlockSpec((1, n), lambda h, i: (0, 0))
    return pl.pallas_call(
        functools.partial(_diff_body, bq=bq, bkv=bkv, seqs=seqs, lambda_init=lambda_init),
        grid=(nh, t // bq),
        in_specs=[pl.BlockSpec((bq, hw), lambda h, i: (i, h)),
                  _kv_window_spec(seqs, bq, hw, lambda h: (nh + h) * hw),
                  _kv_window_spec(seqs, bq, hw, lambda h: (2 * nh + h) * hw),
                  pl.BlockSpec((1, 1, LANES), lambda h, i: (h, 0, 0)),
                  pl.BlockSpec((1, 3, bq, bq), lambda h, i: (h, 0, 0, 0)),
                  vec_spec(HEAD_DIM), vec_spec(HEAD_DIM), vec_spec(HEAD_DIM), vec_spec(HEAD_DIM), vec_spec(hw)],
        out_specs=pl.BlockSpec((bq, hw), lambda h, i: (i, h)),
        out_shape=jax.ShapeDtypeStruct((t, nh * hw), BF16),
        scratch_shapes=_flash_scratch(2 * bq, bkv, hw, 2) + [pltpu.VMEM((2 * bq, LANES), F32)],
        compiler_params=_params("parallel", "arbitrary"),
        name="diff_attention",
    )(qkv, qkv, qkv, jnp.broadcast_to(slopes.astype(F32)[:, None, None], (nh, 1, LANES)), tabs,
      vec(lq1), vec(lk1), vec(lq2), vec(lk2), vec(subln_gain))


def _dilated_body(q_ref, kp_ref, kc_ref, kn_ref, vp_ref, vc_ref, vn_ref, o_ref, lse_ref, *, nh, dilation, radius, seqs):
    n = pl.program_id(1)
    qb, hb = C_Q_BLOCK, C_HALO_BLOCK
    row0 = n * qb
    seq_start, seq_len = seqs.bounds(row0, dilation)
    kw = qb + 2 * radius
    r_i = lax.broadcasted_iota(jnp.int32, (qb, kw), 0)
    c_i = lax.broadcasted_iota(jnp.int32, (qb, kw), 1)
    jrel = c_i - radius - r_i
    kabs = row0 - radius + c_i
    valid = (jnp.abs(jrel) <= radius) & (kabs >= seq_start) & (kabs < seq_start + seq_len)
    dist = (dilation * jnp.abs(jrel)).astype(F32)
    lane = lax.broadcasted_iota(jnp.int32, (qb, LANES), 1)
    lse_tile = jnp.zeros((qb, LANES), F32)
    sls = [slice(h * HEAD_DIM, (h + 1) * HEAD_DIM) for h in range(nh)]
    window = lambda p_ref, c_ref, n_ref, sl: jnp.concatenate([p_ref[hb - radius:, sl], c_ref[:, sl], n_ref[:radius, sl]],
                                                             axis=0)
    ss = [lax.dot_general(q_ref[:, sl], window(kp_ref, kc_ref, kn_ref, sl), NT_DIMS, preferred_element_type=F32)
          for sl in sls]
    for h, (sl, s) in enumerate(zip(sls, ss)):
        slope = LOG2E * 2.0 ** (-8.0 * (h + 1) / nh)
        v = window(vp_ref, vc_ref, vn_ref, sl)
        s = jnp.where(valid, s - slope * dist, MASK_VALUE)
        m = jnp.max(s, axis=-1, keepdims=True)
        p = jnp.exp2(s - m)
        l = jnp.sum(p, axis=-1, keepdims=True)
        o_ref[:, sl] = jnp.dot(p.astype(BF16), v, preferred_element_type=F32) / l
        lse_tile = jnp.where(lane == h, m + jnp.log2(l), lse_tile)
    lse_ref[...] = lse_tile


def residue_major(a, dilation):
    t = a.shape[0]
    return a if dilation == 1 else a.reshape(t // dilation, dilation, -1).swapaxes(0, 1).reshape(t, -1)


def dilated_group(qkv, nh, window, dilation, seqs):
    t, w = qkv.shape
    hw = nh * HEAD_DIM
    radius = window // (2 * dilation)
    qb, hb = C_Q_BLOCK, C_HALO_BLOCK
    assert radius <= hb and qb % hb == 0 and seqs.s2 % (dilation * qb) == 0
    rows = t // dilation
    nblk, nhalo = rows // qb, rows // hb

    def centre(which):
        return pl.BlockSpec((qb, hw), lambda c, n: (c * nblk + n, which))

    def halo(which, side):
        first = lambda n: n * (qb // hb) - 1 if side == 0 else (n + 1) * (qb // hb)
        return pl.BlockSpec((hb, hw), lambda c, n: (c * nhalo + jnp.clip(first(n), 0, nhalo - 1), which))

    o, lse = pl.pallas_call(
        functools.partial(_dilated_body, nh=nh, dilation=dilation, radius=radius, seqs=seqs),
        grid=(dilation, nblk),
        in_specs=[centre(0), halo(1, 0), centre(1), halo(1, 1), halo(2, 0), centre(2), halo(2, 1)],
        out_specs=[pl.BlockSpec((qb, hw), lambda c, n: (n, c)),
                   pl.BlockSpec((qb, LANES), lambda c, n: (n, c))],
        out_shape=[jax.ShapeDtypeStruct((rows, dilation * hw), F32),
                   jax.ShapeDtypeStruct((rows, dilation * LANES), F32)],
        compiler_params=_params("parallel", "parallel"),
        name=f"dilated_d{dilation}",
    )(qkv, qkv, qkv, qkv, qkv, qkv, qkv)
    return o.reshape(t, hw), lse.reshape(t, LANES)


def _merge_body(*refs, ng, nh):
    o_refs, lse_refs, out_ref = refs[:ng], refs[ng:2 * ng], refs[2 * ng]
    lses = [r[...] for r in lse_refs]
    mx = functools.reduce(jnp.maximum, lses)
    es = [jnp.exp2(x - mx) for x in lses]
    tot = functools.reduce(lambda a, b: a + b, es)
    ws = [e / tot for e in es]
    for h in range(nh):
        sl = slice(h * HEAD_DIM, (h + 1) * HEAD_DIM)
        acc = ws[0][:, h:h + 1] * o_refs[0][:, sl]
        for gi in range(1, ng):
            acc = acc + ws[gi][:, h:h + 1] * o_refs[gi][:, sl]
        out_ref[:, sl] = acc.astype(out_ref.dtype)


def merge_groups(outs, lses, nh):
    t, hw = outs[0].shape
    ng = len(outs)
    bm = _pick(t, 256, 8)
    return pl.pallas_call(
        functools.partial(_merge_body, ng=ng, nh=nh),
        grid=(t // bm,),
        in_specs=[pl.BlockSpec((bm, hw), lambda i: (i, 0))] * ng + [pl.BlockSpec((bm, LANES), lambda i: (i, 0))] * ng,
        out_specs=pl.BlockSpec((bm, hw), lambda i: (i, 0)),
        out_shape=jax.ShapeDtypeStruct((t, hw), BF16),
        compiler_params=_params("parallel"),
        name="dilated_merge",
    )(*outs, *lses)


def _na_block_maps(seqs):
    def maps(rb):
        r0 = rb * NA_BLOCK_ROWS
        start, nrows = seqs.bounds(r0, GRID_W)
        rl = r0 - start
        ws = start + jnp.clip(rl - NA_ROWS // 2, 0, nrows - NA_WIN_ROWS)
        variant = jnp.where(rl == 0, 0, jnp.where(rl == nrows - NA_BLOCK_ROWS, 2, 1))
        return ws, variant

    return maps


def _na_body(q_ref, k_ref, v_ref, b_ref, o_ref, *, nh):
    sls = [slice(h * HEAD_DIM, (h + 1) * HEAD_DIM) for h in range(nh)]
    ss = [lax.dot_general(q_ref[:, sl], k_ref[:, sl], NT_DIMS, preferred_element_type=F32) + b_ref[0, h]
          for h, sl in enumerate(sls)]
    for sl, s in zip(sls, ss):
        m = jnp.max(s, axis=-1, keepdims=True)
        p = jnp.exp2(s - m)
        l = jnp.sum(p, axis=-1, keepdims=True)
        o_ref[:, sl] = (jnp.dot(p.astype(BF16), v_ref[:, sl], preferred_element_type=F32) / l).astype(o_ref.dtype)


def na_bias_tables(rpb):
    c = jnp.arange(GRID_W)
    cs = jnp.clip(c - NA_COLS // 2, 0, GRID_W - NA_COLS)
    col_ok = (c[None, :] >= cs[:, None]) & (c[None, :] < cs[:, None] + NA_COLS)
    col_idx = jnp.clip(c[None, :] - c[:, None] + NA_COLS - 1, 0, 2 * NA_COLS - 2)
    rpb_c = jnp.where(col_ok[None, None], rpb.astype(F32)[:, :, col_idx] * LOG2E, MASK_VALUE)
    q = np.arange(NA_BLOCK_ROWS)
    half = NA_ROWS // 2
    variants = [(np.zeros_like(q), q - half),
                (q, np.zeros_like(q)),
                (np.full_like(q, NA_WIN_ROWS - NA_ROWS), q)]
    kr = np.arange(NA_WIN_ROWS)
    tabs = []
    for off, e in variants:
        rr = kr[None, :] - off[:, None]
        valid = (rr >= 0) & (rr < NA_ROWS)
        row_off = np.clip(rr + half - 1 - e[:, None], 0, 2 * NA_ROWS - 2)
        tab = jnp.where(jnp.asarray(valid)[None, :, :, None, None], rpb_c[:, row_off], MASK_VALUE)
        tabs.append(jnp.transpose(tab, (0, 1, 3, 2, 4)).reshape(rpb.shape[0], NA_BLOCK_ROWS * GRID_W,
                                                               NA_WIN_ROWS * GRID_W))
    return jnp.stack(tabs)


def neighbourhood_attention(qkv, bias, nh, seqs):
    t, w = qkv.shape
    hw = nh * HEAD_DIM
    hps = min(NA_HEADS_PER_STEP, nh)
    gw = hps * HEAD_DIM
    assert seqs.s2 % (NA_BLOCK_ROWS * GRID_W) == 0 and seqs.s2 >= NA_WIN_ROWS * GRID_W
    maps = _na_block_maps(seqs)
    bq = NA_BLOCK_ROWS * GRID_W
    kw = NA_WIN_ROWS * GRID_W

    def win_spec(col0):
        return pl.BlockSpec((pl.Element(kw), pl.Element(gw)),
                            lambda g, rb: (pl.multiple_of(maps(rb)[0] * GRID_W, GRID_W), pl.multiple_of(col0 + g * gw, LANES)))

    return pl.pallas_call(
        functools.partial(_na_body, nh=hps),
        grid=(nh // hps, t // bq),
        in_specs=[pl.BlockSpec((bq, gw), lambda g, rb: (rb, g)),
                  win_spec(hw), win_spec(2 * hw),
                  pl.BlockSpec((1, hps, bq, kw), lambda g, rb: (maps(rb)[1], g, 0, 0))],
        out_specs=pl.BlockSpec((bq, gw), lambda g, rb: (rb, g)),
        out_shape=jax.ShapeDtypeStruct((t, hw), BF16),
        compiler_params=_params("parallel", "arbitrary"),
        name="neighbourhood_attention",
    )(qkv, qkv, qkv, bias)


def _fold_qkv(w, gain, widths, scale):
    col_scale = np.concatenate([np.full((wd,), scale if is_q else 1.0, np.float32) for is_q, wd in widths])
    return (w * (gain.astype(F32)[:, None] * col_scale[None, :])).astype(BF16)


def kernel(x_prompt, x_sample, norm_mix, norm_ffn, norm_final, a_w_qkv, a_q_gain, a_k_gain, a_w_o, b_w_qkv,
           b_lambda_q1, b_lambda_k1, b_lambda_q2, b_lambda_k2, b_subln_gain, b_w_o, c_w_qkv, c_w_o, d_w_qkv, d_rpb,
           d_w_o, ffn_w_gate, ffn_w_up, ffn_w_down):
    b1, s1, d = x_prompt.shape
    b2, s2, _ = x_sample.shape
    seqs = Seqs(b1, s1, b2, s2)
    depth = norm_mix.shape[0]

    a_nk = (a_w_qkv.shape[1] - d) // (2 * HEAD_DIM)
    a_nq = d // HEAD_DIM
    b_nh = b_w_o.shape[0] // (2 * HEAD_DIM)
    c_nh = c_w_o.shape[0] // HEAD_DIM
    d_nh = d_w_o.shape[0] // HEAD_DIM
    ng = len(C_CONFIGS)
    chw = c_nh * HEAD_DIM
    qscale = LOG2E * HEAD_DIM ** -0.5
    qkv_cols = {0: [(False, a_w_qkv.shape[1])],
                1: [(True, b_nh * 2 * HEAD_DIM), (False, 2 * b_nh * 2 * HEAD_DIM)],
                2: [(True, chw), (False, 2 * chw)] * ng,
                3: [(True, d_nh * HEAD_DIM), (False, 2 * d_nh * HEAD_DIM)]}
    w_qkv = {0: a_w_qkv, 1: b_w_qkv, 2: c_w_qkv, 3: d_w_qkv}
    w_o = {0: a_w_o, 1: b_w_o, 2: c_w_o, 3: d_w_o}

    cos, sin = rope_tables(seqs)
    slopes_b = 2.0 ** (-8.0 * jnp.arange(1, b_nh + 1, dtype=F32) / b_nh)
    na_bias = na_bias_tables(d_rpb)

    x, xb, rstd = prep_stream(x_sample.reshape(b2 * s2, d), x_prompt.reshape(b1 * s1, d))
    for i in range(depth):
        kind = i % 4
        fold = lambda w, cols: _fold_qkv(w, norm_mix[i], cols, qscale)
        wq = fold(w_qkv[kind], qkv_cols[kind]) if kind != 2 else None
        if kind == 0:
            qkv = matmul_scaled(xb, wq, rstd, F32)
            q, k, v1 = rope_qk(qkv, cos, sin, a_q_gain, a_k_gain, a_nq, a_nk, seqs)
            o = gqa_attention(q, k, v1, seqs)
        elif kind == 1:
            qkv = matmul_scaled(xb, wq, rstd, BF16)
            lambda_init = 0.8 - 0.6 * math.exp(-0.3 * i)
            o = diff_attention(qkv, slopes_b, b_lambda_q1, b_lambda_k1, b_lambda_q2, b_lambda_k2, b_subln_gain, seqs,
                               lambda_init)
        elif kind == 2:
            group_w = lambda g: fold(c_w_qkv[:, g * 3 * chw:(g + 1) * 3 * chw], qkv_cols[kind][:2])
            group_qkv = lambda g, dil: matmul_scaled(residue_major(xb, dil), group_w(g), residue_major(rstd, dil), BF16)
            outs, lses = zip(*[dilated_group(group_qkv(g, dil), c_nh, win, dil, seqs)
                               for g, (win, dil) in enumerate(C_CONFIGS)])
            o = merge_groups(outs, lses, c_nh)
        else:
            qkv = matmul_scaled(xb, wq, rstd, BF16)
            o = neighbourhood_attention(qkv, na_bias, d_nh, seqs)
        x, xb, rstd = matmul_res(o, w_o[kind].astype(BF16), x)
        gain = norm_ffn[i].astype(F32)[:, None]
        a = gateup(xb, (ffn_w_gate[i] * gain).astype(BF16), (ffn_w_up[i] * gain).astype(BF16), rstd)
        x, xb, rstd = matmul_res(a, ffn_w_down[i].astype(BF16), x, bm_pref=512)

    y_sample = rmsnorm(x, norm_final, F32, row0=0, rows=b2 * s2).reshape(b2, s2, d)
    y_prompt = rmsnorm(x, norm_final, F32, row0=b2 * s2, rows=b1 * s1).reshape(b1, s1, d)
    return y_prompt, y_sample
```

```python
import functools
import math

import jax
import jax.numpy as jnp
import numpy as np
from jax import lax
from jax.experimental import pallas as pl
from jax.experimental.pallas import tpu as pltpu

HEAD_DIM = 128
GRID_W = 64
C_Q_BLOCK = 256
C_HALO_BLOCK = 128
RMS_EPS = 1e-6
MASK_VALUE = -1e30
ROPE_THETA = 10000.0
C_CONFIGS = ((128, 1), (512, 4), (2048, 16))
NA_ROWS = 8
NA_COLS = 16
NA_BLOCK_ROWS = 4
NA_WIN_ROWS = NA_BLOCK_ROWS + NA_ROWS
NA_HEADS_PER_STEP = 8
LOG2E = math.log2(math.e)
LANES = 128
V7X_VMEM_LIMIT_BYTES = 56 * 1024 * 1024

F32 = jnp.float32
BF16 = jnp.bfloat16
NT_DIMS = (((1,), (1,)), ((), ()))


def _params(*sem):
    return pltpu.CompilerParams(dimension_semantics=sem, vmem_limit_bytes=V7X_VMEM_LIMIT_BYTES)


def _pick(n, pref, align=LANES):
    if n <= pref:
        return n
    b = (pref // align) * align
    while b >= align:
        if n % b == 0:
            return b
        b -= align
    raise ValueError(f"no block for {n} under {pref}")


def _rmsnorm_body(x_ref, g_ref, o_ref):
    x = x_ref[...]
    ms = jnp.mean(x * x, axis=-1, keepdims=True)
    o_ref[...] = (x * lax.rsqrt(ms + RMS_EPS) * g_ref[...]).astype(o_ref.dtype)


def rmsnorm(x, gain, out_dtype, row0=0, rows=None):
    t, d = x.shape
    rows = t if rows is None else rows
    bm = _pick(math.gcd(rows, row0) if row0 else rows, 256, 8)
    off = row0 // bm
    return pl.pallas_call(
        _rmsnorm_body,
        grid=(rows // bm,),
        in_specs=[pl.BlockSpec((bm, d), lambda i: (i + off, 0)),
                  pl.BlockSpec((1, d), lambda i: (0, 0))],
        out_specs=pl.BlockSpec((bm, d), lambda i: (i, 0)),
        out_shape=jax.ShapeDtypeStruct((rows, d), out_dtype),
        compiler_params=_params("parallel"),
        name="rmsnorm",
    )(x, gain.reshape(1, d).astype(F32))


def _row_scale(acc, rstd):
    return jnp.concatenate([acc[:, c:c + LANES] * rstd for c in range(0, acc.shape[1], LANES)], axis=1)


def _lane_partial_sumsq(x):
    return sum(x[:, c:c + LANES] * x[:, c:c + LANES] for c in range(0, x.shape[1], LANES))


def _finish_rstd(ssq, d_model):
    total = jnp.sum(ssq, axis=-1, keepdims=True)
    return jnp.broadcast_to(lax.rsqrt(total / d_model + RMS_EPS), ssq.shape)


def _prep_body(xs_ref, xp_ref, x_ref, xb_ref, rstd_ref, *, n_first):
    x = jnp.where(pl.program_id(0) < n_first, xs_ref[...], xp_ref[...])
    x_ref[...] = x
    xb_ref[...] = x.astype(xb_ref.dtype)
    rstd_ref[...] = _finish_rstd(_lane_partial_sumsq(x), x.shape[1])


def prep_stream(x_first, x_second):
    (t1, d), (t2, _) = x_first.shape, x_second.shape
    bm = _pick(math.gcd(t1, t2), 256, 8)
    n1, n2 = t1 // bm, t2 // bm
    row = lambda i: (i, 0)
    return pl.pallas_call(
        functools.partial(_prep_body, n_first=n1),
        grid=(n1 + n2,),
        in_specs=[pl.BlockSpec((bm, d), lambda i: (jnp.minimum(i, n1 - 1), 0)),
                  pl.BlockSpec((bm, d), lambda i: (jnp.maximum(i - n1, 0), 0))],
        out_specs=[pl.BlockSpec((bm, d), row), pl.BlockSpec((bm, d), row), pl.BlockSpec((bm, LANES), row)],
        out_shape=[jax.ShapeDtypeStruct((t1 + t2, d), F32), jax.ShapeDtypeStruct((t1 + t2, d), BF16),
                   jax.ShapeDtypeStruct((t1 + t2, LANES), F32)],
        compiler_params=_params("arbitrary"),
        name="prep_stream",
    )(x_first, x_second)


def _mm_scaled_body(a_ref, b_ref, s_ref, o_ref):
    acc = jnp.dot(a_ref[...], b_ref[...], preferred_element_type=F32)
    o_ref[...] = _row_scale(acc, s_ref[...]).astype(o_ref.dtype)


def matmul_scaled(a, b, rstd, out_dtype, bm_pref=1024, bn_pref=1024):
    m, k = a.shape
    _, n = b.shape
    bm, bn = _pick(m, bm_pref, 8), _pick(n, bn_pref)
    return pl.pallas_call(
        _mm_scaled_body,
        grid=(m // bm, n // bn),
        in_specs=[pl.BlockSpec((bm, k), lambda i, j: (i, 0)),
                  pl.BlockSpec((k, bn), lambda i, j: (0, j)),
                  pl.BlockSpec((bm, LANES), lambda i, j: (i, 0))],
        out_specs=pl.BlockSpec((bm, bn), lambda i, j: (i, j)),
        out_shape=jax.ShapeDtypeStruct((m, n), out_dtype),
        compiler_params=_params("parallel", "parallel"),
        name="matmul",
    )(a, b, rstd)


def _emit_stream(x, first, last, d_model, o_ref, xb_ref, rstd_ref):
    o_ref[...] = x
    xb_ref[...] = x.astype(xb_ref.dtype)
    part = _lane_partial_sumsq(x)

    @pl.when(first)
    def _():
        rstd_ref[...] = part

    @pl.when(jnp.logical_not(first))
    def _():
        rstd_ref[...] += part

    @pl.when(last)
    def _():
        rstd_ref[...] = _finish_rstd(rstd_ref[...], d_model)


def _stream_out(m, n, bm, bn, idx):
    specs = [pl.BlockSpec((bm, bn), idx), pl.BlockSpec((bm, bn), idx),
             pl.BlockSpec((bm, LANES), lambda i, *_: (i, 0))]
    shapes = [jax.ShapeDtypeStruct((m, n), F32), jax.ShapeDtypeStruct((m, n), BF16),
              jax.ShapeDtypeStruct((m, LANES), F32)]
    return specs, shapes


def _mm_res_body(a_ref, b_ref, r_ref, o_ref, xb_ref, rstd_ref, *, d_model):
    j = pl.program_id(1)
    x = jnp.dot(a_ref[...], b_ref[...], preferred_element_type=F32) + r_ref[...]
    _emit_stream(x, j == 0, j == pl.num_programs(1) - 1, d_model, o_ref, xb_ref, rstd_ref)


def matmul_res(a, b, residual, bm_pref=1024, bn_pref=512):
    m, k = a.shape
    _, n = b.shape
    bm, bn = _pick(m, bm_pref, 8), _pick(n, bn_pref)
    out_specs, out_shape = _stream_out(m, n, bm, bn, lambda i, j: (i, j))
    return pl.pallas_call(
        functools.partial(_mm_res_body, d_model=n),
        grid=(m // bm, n // bn),
        in_specs=[pl.BlockSpec((bm, k), lambda i, j: (i, 0)),
                  pl.BlockSpec((k, bn), lambda i, j: (0, j)),
                  pl.BlockSpec((bm, bn), lambda i, j: (i, j))],
        out_specs=out_specs,
        out_shape=out_shape,
        compiler_params=_params("parallel", "arbitrary"),
        name="matmul_res",
    )(a, b, residual)


def _gateup_body(h_ref, wg_ref, wu_ref, s_ref, o_ref):
    h = h_ref[...]
    rstd = s_ref[...]
    g = _row_scale(jnp.dot(h, wg_ref[...], preferred_element_type=F32), rstd)
    u = _row_scale(jnp.dot(h, wu_ref[...], preferred_element_type=F32), rstd)
    o_ref[...] = (g / (1.0 + jnp.exp(-g)) * u).astype(o_ref.dtype)


def gateup(h, wg, wu, rstd, bm_pref=1024, bn_pref=512):
    m, k = h.shape
    _, n = wg.shape
    bm, bn = _pick(m, bm_pref, 8), min(bn_pref, n)
    return pl.pallas_call(
        _gateup_body,
        grid=(m // bm, pl.cdiv(n, bn)),
        in_specs=[pl.BlockSpec((bm, k), lambda i, j: (i, 0)),
                  pl.BlockSpec((k, bn), lambda i, j: (0, j)),
                  pl.BlockSpec((k, bn), lambda i, j: (0, j)),
                  pl.BlockSpec((bm, LANES), lambda i, j: (i, 0))],
        out_specs=pl.BlockSpec((bm, bn), lambda i, j: (i, j)),
        out_shape=jax.ShapeDtypeStruct((m, n), BF16),
        compiler_params=_params("parallel", "parallel"),
        name="gateup",
    )(h, wg, wu, rstd)


class Seqs:
    def __init__(self, b1, s1, b2, s2):
        assert b1 == 1 and s1 == 2 * s2, "layout assumes one prompt of twice the sample length"
        self.s1, self.s2, self.nb2 = s1, s2, b2
        self.p0 = b2 * s2
        self.t = self.p0 + s1

    def bounds(self, r0, unit=1):
        p0, s1, s2 = self.p0 // unit, self.s1 // unit, self.s2 // unit
        is_p = r0 >= p0
        return jnp.where(is_p, p0, r0 // s2 * s2), jnp.where(is_p, s1, s2)


def _swap_quarters(y):
    lane = lax.broadcasted_iota(jnp.int32, y.shape, 1)
    first = (lane % (HEAD_DIM // 2)) < (HEAD_DIM // 4)
    return jnp.where(first, pltpu.roll(y, HEAD_DIM - HEAD_DIM // 4, 1), pltpu.roll(y, HEAD_DIM // 4, 1))


def _rope_body(x_ref, cos_ref, sin_ref, qg_ref, kg_ref, q_ref, k_ref, v_ref, *, nq, nk, scale):
    cos, sin = cos_ref[...], sin_ref[...]

    def norm_rope(x, gain):
        ms = jnp.mean(x * x, axis=-1, keepdims=True)
        y = x * lax.rsqrt(ms + RMS_EPS) * gain
        return y * cos + _swap_quarters(y) * sin

    for h in range(nq):
        sl = slice(h * HEAD_DIM, (h + 1) * HEAD_DIM)
        q_ref[:, sl] = (norm_rope(x_ref[:, sl], qg_ref[...]) * scale).astype(q_ref.dtype)
    ones = jnp.ones((x_ref.shape[0], HEAD_DIM), v_ref.dtype)
    for h in range(nk):
        src = slice((nq + h) * HEAD_DIM, (nq + h + 1) * HEAD_DIM)
        k_ref[:, h * HEAD_DIM:(h + 1) * HEAD_DIM] = norm_rope(x_ref[:, src], kg_ref[...]).astype(k_ref.dtype)
        vsrc = slice((nq + nk + h) * HEAD_DIM, (nq + nk + h + 1) * HEAD_DIM)
        v_ref[:, 2 * h * HEAD_DIM:(2 * h + 1) * HEAD_DIM] = x_ref[:, vsrc].astype(v_ref.dtype)
        v_ref[:, (2 * h + 1) * HEAD_DIM:(2 * h + 2) * HEAD_DIM] = ones


def rope_qk(qkv, cos, sin, q_gain, k_gain, nq, nk, seqs):
    t, w = qkv.shape
    bm = _pick(seqs.s2, 256, 8)
    row = lambda i: (i, 0)
    pos = lambda i: (i - seqs.bounds(i * bm)[0] // bm, 0)
    fixed = lambda i: (0, 0)
    return pl.pallas_call(
        functools.partial(_rope_body, nq=nq, nk=nk, scale=LOG2E * HEAD_DIM ** -0.5),
        grid=(t // bm,),
        in_specs=[pl.BlockSpec((bm, w), row), pl.BlockSpec((bm, HEAD_DIM), pos), pl.BlockSpec((bm, HEAD_DIM), pos),
                  pl.BlockSpec((1, HEAD_DIM), fixed), pl.BlockSpec((1, HEAD_DIM), fixed)],
        out_specs=[pl.BlockSpec((bm, nq * HEAD_DIM), row), pl.BlockSpec((bm, nk * HEAD_DIM), row),
                   pl.BlockSpec((bm, 2 * nk * HEAD_DIM), row)],
        out_shape=[jax.ShapeDtypeStruct((t, nq * HEAD_DIM), BF16), jax.ShapeDtypeStruct((t, nk * HEAD_DIM), BF16),
                   jax.ShapeDtypeStruct((t, 2 * nk * HEAD_DIM), BF16)],
        compiler_params=_params("parallel"),
        name="rope_qk",
    )(qkv, cos, sin, q_gain.reshape(1, -1).astype(F32), k_gain.reshape(1, -1).astype(F32))


def rope_tables(seqs):
    half = HEAD_DIM // 2
    inv = ROPE_THETA ** (-jnp.arange(0, half, 2, dtype=F32) / half)
    t = jnp.arange(seqs.s1)
    ang_r = (t // GRID_W).astype(F32)[:, None] * inv
    ang_c = (t % GRID_W).astype(F32)[:, None] * inv
    cr, sr, cc, sc = jnp.cos(ang_r), jnp.sin(ang_r), jnp.cos(ang_c), jnp.sin(ang_c)
    return jnp.concatenate([cr, cr, cc, cc], axis=-1), jnp.concatenate([-sr, sr, -sc, sc], axis=-1)


def _kv_window_spec(seqs, bq, width, col0):
    return pl.BlockSpec((pl.Element(seqs.s1), pl.Element(width)),
                        lambda h, i: (pl.multiple_of(seqs.bounds(i * bq)[0], bq), pl.multiple_of(col0(h), LANES)))


GQA_ROW_CHUNK = 32
DIFF_ROW_CHUNK = 16


def _flash_pipeline(nblk, scores, probs, pv_scale, s_bufs, p_bufs):
    (s_e, s_o), (p_e, p_o) = s_bufs, p_bufs
    scores(0, s_e)
    scores(1, s_o)
    probs(0, s_e, p_e)

    def pair(jj, carry):
        j = 2 * jj + 1
        scores(j + 1, s_e)
        probs(j, s_o, p_o)
        pv_scale(j - 1, p_e, True)
        scores(j + 2, s_o)
        probs(j + 1, s_e, p_e)
        pv_scale(j, p_o, True)
        return carry

    lax.fori_loop(0, (nblk - 2) // 2, pair, 0)
    probs(nblk - 1, s_o, p_o)
    pv_scale(nblk - 2, p_e, True)
    pv_scale(nblk - 1, p_o, False)


def _flash_scratch(m_rows, bkv, acc_width, n_stats):
    return ([pltpu.VMEM((m_rows, bkv), F32)] * 2 + [pltpu.VMEM((m_rows, bkv), BF16)] * 2
            + [pltpu.VMEM((m_rows, acc_width), F32)] + [pltpu.VMEM((m_rows, 1), F32)] * n_stats)


def _gqa_body(q_ref, k_ref, v_ref, o_ref, s_e, s_o, p_e, p_o, acc_ref, m_ref, alpha_ref, *, rep, bq, bkv, seqs):
    i = pl.program_id(1)
    nblk = seqs.bounds(i * bq)[1] // bkv
    q = q_ref[...]
    qs = jnp.concatenate([q[:, r * HEAD_DIM:(r + 1) * HEAD_DIM] for r in range(rep)], axis=0)
    rows = lambda j: pl.ds(pl.multiple_of(j * bkv, bkv), bkv)

    def scores(j, s_ref):
        s_ref[...] = lax.dot_general(qs, k_ref[rows(j), :], NT_DIMS, preferred_element_type=F32)

    def probs(j, s_ref, p_ref):
        for r0 in range(0, rep * bq, GQA_ROW_CHUNK):
            rs = slice(r0, r0 + GQA_ROW_CHUNK)
            s = s_ref[rs, :]
            m = m_ref[rs, :]
            m_new = jnp.maximum(m, jnp.max(s, axis=-1, keepdims=True))
            p_ref[rs, :] = jnp.exp2(s - m_new).astype(BF16)
            alpha_ref[rs, :] = jnp.exp2(m - m_new)
            m_ref[rs, :] = m_new

    def pv_scale(j, p_ref, rescale):
        acc = acc_ref[...] + jnp.dot(p_ref[...], v_ref[rows(j), :], preferred_element_type=F32)
        acc_ref[...] = acc * alpha_ref[...] if rescale else acc

    acc_ref[...] = jnp.zeros_like(acc_ref)
    m_ref[...] = jnp.full_like(m_ref, MASK_VALUE)
    _flash_pipeline(nblk, scores, probs, pv_scale, (s_e, s_o), (p_e, p_o))
    acc = acc_ref[...]
    o = acc[:, :HEAD_DIM] / acc[:, HEAD_DIM:]
    for r in range(rep):
        o_ref[:, r * HEAD_DIM:(r + 1) * HEAD_DIM] = o[r * bq:(r + 1) * bq].astype(o_ref.dtype)


def gqa_attention(q, k, v1, seqs, bq_pref=256, bkv_pref=1024):
    t, wq = q.shape
    nk = k.shape[1] // HEAD_DIM
    rep = wq // HEAD_DIM // nk
    bq = _pick(seqs.s2, bq_pref, 16)
    bkv = _pick(seqs.s2 // 2, bkv_pref)
    return pl.pallas_call(
        functools.partial(_gqa_body, rep=rep, bq=bq, bkv=bkv, seqs=seqs),
        grid=(nk, t // bq),
        in_specs=[pl.BlockSpec((bq, rep * HEAD_DIM), lambda g, i: (i, g)),
                  _kv_window_spec(seqs, bq, HEAD_DIM, lambda g: g * HEAD_DIM),
                  _kv_window_spec(seqs, bq, 2 * HEAD_DIM, lambda g: g * 2 * HEAD_DIM)],
        out_specs=pl.BlockSpec((bq, rep * HEAD_DIM), lambda g, i: (i, g)),
        out_shape=jax.ShapeDtypeStruct((t, wq), BF16),
        scratch_shapes=_flash_scratch(rep * bq, bkv, 2 * HEAD_DIM, 2),
        compiler_params=_params("parallel", "arbitrary"),
        name="gqa_attention",
    )(q, k, v1)


def _diff_body(q_ref, k_ref, v_ref, slope_ref, tab_ref, lq1_ref, lk1_ref, lq2_ref, lk2_ref, g_ref, o_ref,
               s_e, s_o, p_e, p_o, acc_ref, m_ref, alpha_ref, l_ref, *, bq, bkv, seqs, lambda_init):
    i = pl.program_id(1)
    start, slen = seqs.bounds(i * bq)
    nblk = slen // bkv
    qpos0 = i * bq - start
    q = q_ref[...]
    q0, q1 = q[:, :HEAD_DIM], q[:, HEAD_DIM:]
    slope = slope_ref[0][:, :1] * LOG2E
    rows = lambda j: pl.ds(pl.multiple_of(j * bkv, bkv), bkv)

    def scores(j, s_ref):
        k = k_ref[rows(j), :]
        s_ref[:bq] = lax.dot_general(q0, k[:, :HEAD_DIM], NT_DIMS, preferred_element_type=F32)
        s_ref[bq:] = lax.dot_general(q1, k[:, HEAD_DIM:], NT_DIMS, preferred_element_type=F32)

    def probs(j, s_ref, p_ref):
        subs = []
        for c0 in range(0, bkv, bq):
            lead = qpos0 - (j * bkv + c0)
            kind = jnp.where(lead == 0, 2, jnp.where(lead > 0, 0, 1))
            subs.append((c0, kind, slope * jnp.abs(lead).astype(F32)))
        for r0 in range(0, 2 * bq, DIFF_ROW_CHUNK):
            rs = slice(r0, r0 + DIFF_ROW_CHUNK)
            rq = slice(r0 % bq, r0 % bq + DIFF_ROW_CHUNK)
            ss = [s_ref[rs, c0:c0 + bq] - tab_ref[0, kind, rq, :] for c0, kind, _ in subs]
            mx = functools.reduce(jnp.maximum, [jnp.max(s, axis=-1, keepdims=True) - shift
                                                for s, (_, _, shift) in zip(ss, subs)])
            m = m_ref[rs, :]
            m_new = jnp.maximum(m, mx)
            alpha = jnp.exp2(m - m_new)
            lsum = alpha * l_ref[rs, :]
            for s, (c0, _, shift) in zip(ss, subs):
                p = jnp.exp2(s - (m_new + shift))
                p_ref[rs, c0:c0 + bq] = p.astype(BF16)
                lsum = lsum + sum(p[:, c:c + LANES] for c in range(0, bq, LANES))
            l_ref[rs, :] = lsum
            alpha_ref[rs, :] = alpha
            m_ref[rs, :] = m_new

    def pv_scale(j, p_ref, rescale):
        acc = acc_ref[...] + jnp.dot(p_ref[...], v_ref[rows(j), :], preferred_element_type=F32)
        acc_ref[...] = acc * alpha_ref[...] if rescale else acc

    acc_ref[...] = jnp.zeros_like(acc_ref)
    m_ref[...] = jnp.full_like(m_ref, MASK_VALUE)
    l_ref[...] = jnp.zeros_like(l_ref)
    _flash_pipeline(nblk, scores, probs, pv_scale, (s_e, s_o), (p_e, p_o))
    o = acc_ref[...] / jnp.sum(l_ref[...], axis=-1, keepdims=True)
    lam = (jnp.exp(jnp.sum(lq1_ref[...] * lk1_ref[...], axis=-1, keepdims=True))
           - jnp.exp(jnp.sum(lq2_ref[...] * lk2_ref[...], axis=-1, keepdims=True)) + lambda_init)
    d = o[:bq] - lam * o[bq:]
    ms = jnp.mean(d * d, axis=-1, keepdims=True)
    o_ref[...] = (d * lax.rsqrt(ms + RMS_EPS) * g_ref[...] * (1.0 - lambda_init)).astype(o_ref.dtype)


def diff_attention(qkv, slopes, lq1, lk1, lq2, lk2, subln_gain, seqs, lambda_init, bq_pref=512, bkv_pref=1024):
    t, w = qkv.shape
    hw = 2 * HEAD_DIM
    nh = w // (3 * hw)
    bq = _pick(seqs.s2 // 2, bq_pref)
    bkv = _pick(seqs.s2 // 2, bkv_pref)
    assert bkv % bq == 0
    rel = (jnp.arange(bq)[:, None] - jnp.arange(bq)[None, :]).astype(F32) * (LOG2E * slopes.astype(F32))[:, None, None]
    tabs = jnp.stack([rel, -rel, jnp.abs(rel)], axis=1)
    vec = lambda a: a.reshape(1, -1).astype(F32)
    vec_spec = lambda n: pl.BlockSpec((1, n), lambda h, i: (0, 0))
    return pl.pallas_call(
        functools.partial(_diff_body, bq=bq, bkv=bkv, seqs=seqs, lambda_init=lambda_init),
        grid=(nh, t // bq),
        in_specs=[pl.BlockSpec((bq, hw), lambda h, i: (i, h)),
                  _kv_window_spec(seqs, bq, hw, lambda h: (nh + h) * hw),
                  _kv_window_spec(seqs, bq, hw, lambda h: (2 * nh + h) * hw),
                  pl.BlockSpec((1, 1, LANES), lambda h, i: (h, 0, 0)),
                  pl.BlockSpec((1, 3, bq, bq), lambda h, i: (h, 0, 0, 0)),
                  vec_spec(HEAD_DIM), vec_spec(HEAD_DIM), vec_spec(HEAD_DIM), vec_spec(HEAD_DIM), vec_spec(hw)],
        out_specs=pl.BlockSpec((bq, hw), lambda h, i: (i, h)),
        out_shape=jax.ShapeDtypeStruct((t, nh * hw), BF16),
        scratch_shapes=_flash_scratch(2 * bq, bkv, hw, 2) + [pltpu.VMEM((2 * bq, LANES), F32)],
        compiler_params=_params("parallel", "arbitrary"),
        name="diff_attention",
    )(qkv, qkv, qkv, jnp.broadcast_to(slopes.astype(F32)[:, None, None], (nh, 1, LANES)), tabs,
      vec(lq1), vec(lk1), vec(lq2), vec(lk2), vec(subln_gain))


def _dilated_body(q_ref, kp_ref, kc_ref, kn_ref, vp_ref, vc_ref, vn_ref, o_ref, lse_ref, *, nh, dilation, radius, seqs):
    n = pl.program_id(1)
    qb, hb = C_Q_BLOCK, C_HALO_BLOCK
    row0 = n * qb
    seq_start, seq_len = seqs.bounds(row0, dilation)
    kw = qb + 2 * radius
    r_i = lax.broadcasted_iota(jnp.int32, (qb, kw), 0)
    c_i = lax.broadcasted_iota(jnp.int32, (qb, kw), 1)
    jrel = c_i - radius - r_i
    kabs = row0 - radius + c_i
    valid = (jnp.abs(jrel) <= radius) & (kabs >= seq_start) & (kabs < seq_start + seq_len)
    dist = (dilation * jnp.abs(jrel)).astype(F32)
    lane = lax.broadcasted_iota(jnp.int32, (qb, LANES), 1)
    lse_tile = jnp.zeros((qb, LANES), F32)
    sls = [slice(h * HEAD_DIM, (h + 1) * HEAD_DIM) for h in range(nh)]
    window = lambda p_ref, c_ref, n_ref, sl: jnp.concatenate([p_ref[hb - radius:, sl], c_ref[:, sl], n_ref[:radius, sl]],
                                                             axis=0)
    score = lambda h: lax.dot_general(q_ref[:, sls[h]], window(kp_ref, kc_ref, kn_ref, sls[h]), NT_DIMS,
                                      preferred_element_type=F32)
    s_next = score(0)
    for h, sl in enumerate(sls):
        s = s_next
        if h + 1 < nh:
            s_next = score(h + 1)
        slope = LOG2E * 2.0 ** (-8.0 * (h + 1) / nh)
        v = window(vp_ref, vc_ref, vn_ref, sl)
        s = jnp.where(valid, s - slope * dist, MASK_VALUE)
        m = jnp.max(s, axis=-1, keepdims=True)
        p = jnp.exp2(s - m)
        l = jnp.sum(p, axis=-1, keepdims=True)
        o_ref[:, sl] = jnp.dot(p.astype(BF16), v, preferred_element_type=F32) / l
        lse_tile = jnp.where(lane == h, m + jnp.log2(l), lse_tile)
    lse_ref[...] = lse_tile


def residue_major(a, dilation):
    t = a.shape[0]
    return a if dilation == 1 else a.reshape(t // dilation, dilation, -1).swapaxes(0, 1).reshape(t, -1)


def dilated_group(qkv, nh, window, dilation, seqs):
    t, w = qkv.shape
    hw = nh * HEAD_DIM
    radius = window // (2 * dilation)
    qb, hb = C_Q_BLOCK, C_HALO_BLOCK
    assert radius <= hb and qb % hb == 0 and seqs.s2 % (dilation * qb) == 0
    rows = t // dilation
    nblk, nhalo = rows // qb, rows // hb

    def centre(which):
        return pl.BlockSpec((qb, hw), lambda c, n: (c * nblk + n, which))

    def halo(which, side):
        first = lambda n: n * (qb // hb) - 1 if side == 0 else (n + 1) * (qb // hb)
        return pl.BlockSpec((hb, hw), lambda c, n: (c * nhalo + jnp.clip(first(n), 0, nhalo - 1), which))

    o, lse = pl.pallas_call(
        functools.partial(_dilated_body, nh=nh, dilation=dilation, radius=radius, seqs=seqs),
        grid=(dilation, nblk),
        in_specs=[centre(0), halo(1, 0), centre(1), halo(1, 1), halo(2, 0), centre(2), halo(2, 1)],
        out_specs=[pl.BlockSpec((qb, hw), lambda c, n: (n, c)),
                   pl.BlockSpec((qb, LANES), lambda c, n: (n, c))],
        out_shape=[jax.ShapeDtypeStruct((rows, dilation * hw), F32),
                   jax.ShapeDtypeStruct((rows, dilation * LANES), F32)],
        compiler_params=_params("parallel", "parallel"),
        name=f"dilated_d{dilation}",
    )(qkv, qkv, qkv, qkv, qkv, qkv, qkv)
    return o.reshape(t, hw), lse.reshape(t, LANES)


def _merge_body(*refs, ng, nh):
    o_refs, lse_refs, out_ref = refs[:ng], refs[ng:2 * ng], refs[2 * ng]
    lses = [r[...] for r in lse_refs]
    mx = functools.reduce(jnp.maximum, lses)
    es = [jnp.exp2(x - mx) for x in lses]
    tot = functools.reduce(lambda a, b: a + b, es)
    ws = [e / tot for e in es]
    for h in range(nh):
        sl = slice(h * HEAD_DIM, (h + 1) * HEAD_DIM)
        acc = ws[0][:, h:h + 1] * o_refs[0][:, sl]
        for gi in range(1, ng):
            acc = acc + ws[gi][:, h:h + 1] * o_refs[gi][:, sl]
        out_ref[:, sl] = acc.astype(out_ref.dtype)


def merge_groups(outs, lses, nh):
    t, hw = outs[0].shape
    ng = len(outs)
    bm = _pick(t, 256, 8)
    return pl.pallas_call(
        functools.partial(_merge_body, ng=ng, nh=nh),
        grid=(t // bm,),
        in_specs=[pl.BlockSpec((bm, hw), lambda i: (i, 0))] * ng + [pl.BlockSpec((bm, LANES), lambda i: (i, 0))] * ng,
        out_specs=pl.BlockSpec((bm, hw), lambda i: (i, 0)),
        out_shape=jax.ShapeDtypeStruct((t, hw), BF16),
        compiler_params=_params("parallel"),
        name="dilated_merge",
    )(*outs, *lses)


def _na_block_maps(seqs):
    def maps(rb):
        r0 = rb * NA_BLOCK_ROWS
        start, nrows = seqs.bounds(r0, GRID_W)
        rl = r0 - start
        ws = start + jnp.clip(rl - NA_ROWS // 2, 0, nrows - NA_WIN_ROWS)
        variant = jnp.where(rl == 0, 0, jnp.where(rl == nrows - NA_BLOCK_ROWS, 2, 1))
        return ws, variant

    return maps


def _na_body(q_ref, k_ref, v_ref, b_ref, o_ref, *, nh):
    sls = [slice(h * HEAD_DIM, (h + 1) * HEAD_DIM) for h in range(nh)]
    score = lambda h: lax.dot_general(q_ref[:, sls[h]], k_ref[:, sls[h]], NT_DIMS, preferred_element_type=F32) + b_ref[0, h]
    s_next = score(0)
    for h, sl in enumerate(sls):
        s = s_next
        if h + 1 < nh:
            s_next = score(h + 1)
        m = jnp.max(s, axis=-1, keepdims=True)
        p = jnp.exp2(s - m)
        l = jnp.sum(p, axis=-1, keepdims=True)
        o_ref[:, sl] = (jnp.dot(p.astype(BF16), v_ref[:, sl], preferred_element_type=F32) / l).astype(o_ref.dtype)


def na_bias_tables(rpb):
    c = jnp.arange(GRID_W)
    cs = jnp.clip(c - NA_COLS // 2, 0, GRID_W - NA_COLS)
    col_ok = (c[None, :] >= cs[:, None]) & (c[None, :] < cs[:, None] + NA_COLS)
    col_idx = jnp.clip(c[None, :] - c[:, None] + NA_COLS - 1, 0, 2 * NA_COLS - 2)
    rpb_c = jnp.where(col_ok[None, None], rpb.astype(F32)[:, :, col_idx] * LOG2E, MASK_VALUE)
    q = np.arange(NA_BLOCK_ROWS)
    half = NA_ROWS // 2
    variants = [(np.zeros_like(q), q - half),
                (q, np.zeros_like(q)),
                (np.full_like(q, NA_WIN_ROWS - NA_ROWS), q)]
    kr = np.arange(NA_WIN_ROWS)
    tabs = []
    for off, e in variants:
        rr = kr[None, :] - off[:, None]
        valid = (rr >= 0) & (rr < NA_ROWS)
        row_off = np.clip(rr + half - 1 - e[:, None], 0, 2 * NA_ROWS - 2)
        tab = jnp.where(jnp.asarray(valid)[None, :, :, None, None], rpb_c[:, row_off], MASK_VALUE)
        tabs.append(jnp.transpose(tab, (0, 1, 3, 2, 4)).reshape(rpb.shape[0], NA_BLOCK_ROWS * GRID_W,
                                                               NA_WIN_ROWS * GRID_W))
    return jnp.stack(tabs)


def neighbourhood_attention(qkv, bias, nh, seqs):
    t, w = qkv.shape
    hw = nh * HEAD_DIM
    hps = min(NA_HEADS_PER_STEP, nh)
    gw = hps * HEAD_DIM
    assert seqs.s2 % (NA_BLOCK_ROWS * GRID_W) == 0 and seqs.s2 >= NA_WIN_ROWS * GRID_W
    maps = _na_block_maps(seqs)
    bq = NA_BLOCK_ROWS * GRID_W
    kw = NA_WIN_ROWS * GRID_W

    def win_spec(col0):
        return pl.BlockSpec((pl.Element(kw), pl.Element(gw)),
                            lambda g, rb: (pl.multiple_of(maps(rb)[0] * GRID_W, GRID_W), pl.multiple_of(col0 + g * gw, LANES)))

    return pl.pallas_call(
        functools.partial(_na_body, nh=hps),
        grid=(nh // hps, t // bq),
        in_specs=[pl.BlockSpec((bq, gw), lambda g, rb: (rb, g)),
                  win_spec(hw), win_spec(2 * hw),
                  pl.BlockSpec((1, hps, bq, kw), lambda g, rb: (maps(rb)[1], g, 0, 0))],
        out_specs=pl.BlockSpec((bq, gw), lambda g, rb: (rb, g)),
        out_shape=jax.ShapeDtypeStruct((t, hw), BF16),
        compiler_params=_params("parallel", "arbitrary"),
        name="neighbourhood_attention",
    )(qkv, qkv, qkv, bias)


def _fold_qkv(w, gain, widths, scale):
    col_scale = np.concatenate([np.full((wd,), scale if is_q else 1.0, np.float32) for is_q, wd in widths])
    return (w * (gain.astype(F32)[:, None] * col_scale[None, :])).astype(BF16)


def kernel(x_prompt, x_sample, norm_mix, norm_ffn, norm_final, a_w_qkv, a_q_gain, a_k_gain, a_w_o, b_w_qkv,
           b_lambda_q1, b_lambda_k1, b_lambda_q2, b_lambda_k2, b_subln_gain, b_w_o, c_w_qkv, c_w_o, d_w_qkv, d_rpb,
           d_w_o, ffn_w_gate, ffn_w_up, ffn_w_down):
    b1, s1, d = x_prompt.shape
    b2, s2, _ = x_sample.shape
    seqs = Seqs(b1, s1, b2, s2)
    depth = norm_mix.shape[0]

    a_nk = (a_w_qkv.shape[1] - d) // (2 * HEAD_DIM)
    a_nq = d // HEAD_DIM
    b_nh = b_w_o.shape[0] // (2 * HEAD_DIM)
    c_nh = c_w_o.shape[0] // HEAD_DIM
    d_nh = d_w_o.shape[0] // HEAD_DIM
    ng = len(C_CONFIGS)
    chw = c_nh * HEAD_DIM
    qscale = LOG2E * HEAD_DIM ** -0.5
    qkv_cols = {0: [(False, a_w_qkv.shape[1])],
                1: [(True, b_nh * 2 * HEAD_DIM), (False, 2 * b_nh * 2 * HEAD_DIM)],
                2: [(True, chw), (False, 2 * chw)] * ng,
                3: [(True, d_nh * HEAD_DIM), (False, 2 * d_nh * HEAD_DIM)]}
    w_qkv = {0: a_w_qkv, 1: b_w_qkv, 2: c_w_qkv, 3: d_w_qkv}
    w_o = {0: a_w_o, 1: b_w_o, 2: c_w_o, 3: d_w_o}

    cos, sin = rope_tables(seqs)
    slopes_b = 2.0 ** (-8.0 * jnp.arange(1, b_nh + 1, dtype=F32) / b_nh)
    na_bias = na_bias_tables(d_rpb)

    x, xb, rstd = prep_stream(x_sample.reshape(b2 * s2, d), x_prompt.reshape(b1 * s1, d))
    for i in range(depth):
        kind = i % 4
        fold = lambda w, cols: _fold_qkv(w, norm_mix[i], cols, qscale)
        wq = fold(w_qkv[kind], qkv_cols[kind]) if kind != 2 else None
        if kind == 0:
            qkv = matmul_scaled(xb, wq, rstd, F32)
            q, k, v1 = rope_qk(qkv, cos, sin, a_q_gain, a_k_gain, a_nq, a_nk, seqs)
            o = gqa_attention(q, k, v1, seqs)
        elif kind == 1:
            qkv = matmul_scaled(xb, wq, rstd, BF16)
            lambda_init = 0.8 - 0.6 * math.exp(-0.3 * i)
            o = diff_attention(qkv, slopes_b, b_lambda_q1, b_lambda_k1, b_lambda_q2, b_lambda_k2, b_subln_gain, seqs,
                               lambda_init)
        elif kind == 2:
            group_w = lambda g: fold(c_w_qkv[:, g * 3 * chw:(g + 1) * 3 * chw], qkv_cols[kind][:2])
            group_qkv = lambda g, dil: matmul_scaled(residue_major(xb, dil), group_w(g), residue_major(rstd, dil), BF16)
            outs, lses = zip(*[dilated_group(group_qkv(g, dil), c_nh, win, dil, seqs)
                               for g, (win, dil) in enumerate(C_CONFIGS)])
            o = merge_groups(outs, lses, c_nh)
        else:
            qkv = matmul_scaled(xb, wq, rstd, BF16)
            o = neighbourhood_attention(qkv, na_bias, d_nh, seqs)
        x, xb, rstd = matmul_res(o, w_o[kind].astype(BF16), x)
        gain = norm_ffn[i].astype(F32)[:, None]
        a = gateup(xb, (ffn_w_gate[i] * gain).astype(BF16), (ffn_w_up[i] * gain).astype(BF16), rstd)
        x, xb, rstd = matmul_res(a, ffn_w_down[i].astype(BF16), x, bm_pref=512)

    y_sample = rmsnorm(x, norm_final, F32, row0=0, rows=b2 * s2).reshape(b2, s2, d)
    y_prompt = rmsnorm(x, norm_final, F32, row0=b2 * s2, rows=b1 * s1).reshape(b1, s1, d)
    return y_prompt, y_sample
```

```python
import functools
import math

import jax
import jax.numpy as jnp
import numpy as np
from jax import lax
from jax.experimental import pallas as pl
from jax.experimental.pallas import tpu as pltpu

HEAD_DIM = 128
GRID_W = 64
C_Q_BLOCK = 256
C_HALO_BLOCK = 128
RMS_EPS = 1e-6
MASK_VALUE = -1e30
ROPE_THETA = 10000.0
C_CONFIGS = ((128, 1), (512, 4), (2048, 16))
NA_ROWS = 8
NA_COLS = 16
NA_BLOCK_ROWS = 4
NA_WIN_ROWS = NA_BLOCK_ROWS + NA_ROWS
NA_HEADS_PER_STEP = 8
LOG2E = math.log2(math.e)
LANES = 128
V7X_VMEM_LIMIT_BYTES = 56 * 1024 * 1024

F32 = jnp.float32
BF16 = jnp.bfloat16
NT_DIMS = (((1,), (1,)), ((), ()))


def _params(*sem):
    return pltpu.CompilerParams(dimension_semantics=sem, vmem_limit_bytes=V7X_VMEM_LIMIT_BYTES)


def _pick(n, pref, align=LANES):
    if n <= pref:
        return n
    b = (pref // align) * align
    while b >= align:
        if n % b == 0:
            return b
        b -= align
    raise ValueError(f"no block for {n} under {pref}")


def _rmsnorm_body(x_ref, g_ref, o_ref):
    x = x_ref[...]
    ms = jnp.mean(x * x, axis=-1, keepdims=True)
    o_ref[...] = (x * lax.rsqrt(ms + RMS_EPS) * g_ref[...]).astype(o_ref.dtype)


def rmsnorm(x, gain, out_dtype, row0=0, rows=None):
    t, d = x.shape
    rows = t if rows is None else rows
    bm = _pick(math.gcd(rows, row0) if row0 else rows, 256, 8)
    off = row0 // bm
    return pl.pallas_call(
        _rmsnorm_body,
        grid=(rows // bm,),
        in_specs=[pl.BlockSpec((bm, d), lambda i: (i + off, 0)),
                  pl.BlockSpec((1, d), lambda i: (0, 0))],
        out_specs=pl.BlockSpec((bm, d), lambda i: (i, 0)),
        out_shape=jax.ShapeDtypeStruct((rows, d), out_dtype),
        compiler_params=_params("parallel"),
        name="rmsnorm",
    )(x, gain.reshape(1, d).astype(F32))


def _row_scale(acc, rstd):
    return jnp.concatenate([acc[:, c:c + LANES] * rstd for c in range(0, acc.shape[1], LANES)], axis=1)


def _lane_partial_sumsq(x):
    return sum(x[:, c:c + LANES] * x[:, c:c + LANES] for c in range(0, x.shape[1], LANES))


def _finish_rstd(ssq, d_model):
    total = jnp.sum(ssq, axis=-1, keepdims=True)
    return jnp.broadcast_to(lax.rsqrt(total / d_model + RMS_EPS), ssq.shape)


def _prep_body(xs_ref, xp_ref, x_ref, xb_ref, rstd_ref, *, n_first):
    x = jnp.where(pl.program_id(0) < n_first, xs_ref[...], xp_ref[...])
    x_ref[...] = x
    xb_ref[...] = x.astype(xb_ref.dtype)
    rstd_ref[...] = _finish_rstd(_lane_partial_sumsq(x), x.shape[1])


def prep_stream(x_first, x_second):
    (t1, d), (t2, _) = x_first.shape, x_second.shape
    bm = _pick(math.gcd(t1, t2), 256, 8)
    n1, n2 = t1 // bm, t2 // bm
    row = lambda i: (i, 0)
    return pl.pallas_call(
        functools.partial(_prep_body, n_first=n1),
        grid=(n1 + n2,),
        in_specs=[pl.BlockSpec((bm, d), lambda i: (jnp.minimum(i, n1 - 1), 0)),
                  pl.BlockSpec((bm, d), lambda i: (jnp.maximum(i - n1, 0), 0))],
        out_specs=[pl.BlockSpec((bm, d), row), pl.BlockSpec((bm, d), row), pl.BlockSpec((bm, LANES), row)],
        out_shape=[jax.ShapeDtypeStruct((t1 + t2, d), F32), jax.ShapeDtypeStruct((t1 + t2, d), BF16),
                   jax.ShapeDtypeStruct((t1 + t2, LANES), F32)],
        compiler_params=_params("arbitrary"),
        name="prep_stream",
    )(x_first, x_second)


def _mm_scaled_body(a_ref, b_ref, s_ref, o_ref):
    acc = jnp.dot(a_ref[...], b_ref[...], preferred_element_type=F32)
    o_ref[...] = _row_scale(acc, s_ref[...]).astype(o_ref.dtype)


def matmul_scaled(a, b, rstd, out_dtype, bm_pref=1024, bn_pref=1024):
    m, k = a.shape
    _, n = b.shape
    bm, bn = _pick(m, bm_pref, 8), _pick(n, bn_pref)
    return pl.pallas_call(
        _mm_scaled_body,
        grid=(n // bn, m // bm),
        in_specs=[pl.BlockSpec((bm, k), lambda j, i: (i, 0)),
                  pl.BlockSpec((k, bn), lambda j, i: (0, j)),
                  pl.BlockSpec((bm, LANES), lambda j, i: (i, 0))],
        out_specs=pl.BlockSpec((bm, bn), lambda j, i: (i, j)),
        out_shape=jax.ShapeDtypeStruct((m, n), out_dtype),
        compiler_params=_params("parallel", "parallel"),
        name="matmul",
    )(a, b, rstd)


def _emit_stream(x, first, last, d_model, o_ref, xb_ref, rstd_ref):
    o_ref[...] = x
    xb_ref[...] = x.astype(xb_ref.dtype)
    part = _lane_partial_sumsq(x)

    @pl.when(first)
    def _():
        rstd_ref[...] = part

    @pl.when(jnp.logical_not(first))
    def _():
        rstd_ref[...] += part

    @pl.when(last)
    def _():
        rstd_ref[...] = _finish_rstd(rstd_ref[...], d_model)


def _stream_out(m, n, bm, bn, idx):
    specs = [pl.BlockSpec((bm, bn), idx), pl.BlockSpec((bm, bn), idx),
             pl.BlockSpec((bm, LANES), lambda i, *_: (i, 0))]
    shapes = [jax.ShapeDtypeStruct((m, n), F32), jax.ShapeDtypeStruct((m, n), BF16),
              jax.ShapeDtypeStruct((m, LANES), F32)]
    return specs, shapes


def _mm_res_body(a_ref, b_ref, r_ref, o_ref, xb_ref, rstd_ref, *, d_model):
    j = pl.program_id(1)
    x = jnp.dot(a_ref[...], b_ref[...], preferred_element_type=F32) + r_ref[...]
    _emit_stream(x, j == 0, j == pl.num_programs(1) - 1, d_model, o_ref, xb_ref, rstd_ref)


def matmul_res(a, b, residual, bm_pref=1024, bn_pref=512):
    m, k = a.shape
    _, n = b.shape
    bm, bn = _pick(m, bm_pref, 8), _pick(n, bn_pref)
    out_specs, out_shape = _stream_out(m, n, bm, bn, lambda i, j: (i, j))
    return pl.pallas_call(
        functools.partial(_mm_res_body, d_model=n),
        grid=(m // bm, n // bn),
        in_specs=[pl.BlockSpec((bm, k), lambda i, j: (i, 0)),
                  pl.BlockSpec((k, bn), lambda i, j: (0, j)),
                  pl.BlockSpec((bm, bn), lambda i, j: (i, j))],
        out_specs=out_specs,
        out_shape=out_shape,
        compiler_params=_params("parallel", "arbitrary"),
        name="matmul_res",
    )(a, b, residual)


def _gateup_body(h_ref, wg_ref, wu_ref, s_ref, o_ref):
    h = h_ref[...]
    rstd = s_ref[...]
    g = _row_scale(jnp.dot(h, wg_ref[...], preferred_element_type=F32), rstd)
    u = _row_scale(jnp.dot(h, wu_ref[...], preferred_element_type=F32), rstd)
    o_ref[...] = (g / (1.0 + jnp.exp(-g)) * u).astype(o_ref.dtype)


def gateup(h, wg, wu, rstd, bm_pref=1024, bn_pref=512):
    m, k = h.shape
    _, n = wg.shape
    bm, bn = _pick(m, bm_pref, 8), min(bn_pref, n)
    return pl.pallas_call(
        _gateup_body,
        grid=(m // bm, pl.cdiv(n, bn)),
        in_specs=[pl.BlockSpec((bm, k), lambda i, j: (i, 0)),
                  pl.BlockSpec((k, bn), lambda i, j: (0, j)),
                  pl.BlockSpec((k, bn), lambda i, j: (0, j)),
                  pl.BlockSpec((bm, LANES), lambda i, j: (i, 0))],
        out_specs=pl.BlockSpec((bm, bn), lambda i, j: (i, j)),
        out_shape=jax.ShapeDtypeStruct((m, n), BF16),
        compiler_params=_params("parallel", "parallel"),
        name="gateup",
    )(h, wg, wu, rstd)


class Seqs:
    def __init__(self, b1, s1, b2, s2):
        assert b1 == 1 and s1 == 2 * s2, "layout assumes one prompt of twice the sample length"
        self.s1, self.s2, self.nb2 = s1, s2, b2
        self.p0 = b2 * s2
        self.t = self.p0 + s1

    def bounds(self, r0, unit=1):
        p0, s1, s2 = self.p0 // unit, self.s1 // unit, self.s2 // unit
        is_p = r0 >= p0
        return jnp.where(is_p, p0, r0 // s2 * s2), jnp.where(is_p, s1, s2)


def _swap_quarters(y):
    lane = lax.broadcasted_iota(jnp.int32, y.shape, 1)
    first = (lane % (HEAD_DIM // 2)) < (HEAD_DIM // 4)
    return jnp.where(first, pltpu.roll(y, HEAD_DIM - HEAD_DIM // 4, 1), pltpu.roll(y, HEAD_DIM // 4, 1))


def _rope_body(x_ref, cos_ref, sin_ref, qg_ref, kg_ref, q_ref, k_ref, v_ref, *, nq, nk, scale):
    cos, sin = cos_ref[...], sin_ref[...]

    def norm_rope(x, gain):
        ms = jnp.mean(x * x, axis=-1, keepdims=True)
        y = x * lax.rsqrt(ms + RMS_EPS) * gain
        return y * cos + _swap_quarters(y) * sin

    for h in range(nq):
        sl = slice(h * HEAD_DIM, (h + 1) * HEAD_DIM)
        q_ref[:, sl] = (norm_rope(x_ref[:, sl], qg_ref[...]) * scale).astype(q_ref.dtype)
    ones = jnp.ones((x_ref.shape[0], HEAD_DIM), v_ref.dtype)
    for h in range(nk):
        src = slice((nq + h) * HEAD_DIM, (nq + h + 1) * HEAD_DIM)
        k_ref[:, h * HEAD_DIM:(h + 1) * HEAD_DIM] = norm_rope(x_ref[:, src], kg_ref[...]).astype(k_ref.dtype)
        vsrc = slice((nq + nk + h) * HEAD_DIM, (nq + nk + h + 1) * HEAD_DIM)
        v_ref[:, 2 * h * HEAD_DIM:(2 * h + 1) * HEAD_DIM] = x_ref[:, vsrc].astype(v_ref.dtype)
        v_ref[:, (2 * h + 1) * HEAD_DIM:(2 * h + 2) * HEAD_DIM] = ones


def rope_qk(qkv, cos, sin, q_gain, k_gain, nq, nk, seqs):
    t, w = qkv.shape
    bm = _pick(seqs.s2, 256, 8)
    row = lambda i: (i, 0)
    pos = lambda i: (i - seqs.bounds(i * bm)[0] // bm, 0)
    fixed = lambda i: (0, 0)
    return pl.pallas_call(
        functools.partial(_rope_body, nq=nq, nk=nk, scale=LOG2E * HEAD_DIM ** -0.5),
        grid=(t // bm,),
        in_specs=[pl.BlockSpec((bm, w), row), pl.BlockSpec((bm, HEAD_DIM), pos), pl.BlockSpec((bm, HEAD_DIM), pos),
                  pl.BlockSpec((1, HEAD_DIM), fixed), pl.BlockSpec((1, HEAD_DIM), fixed)],
        out_specs=[pl.BlockSpec((bm, nq * HEAD_DIM), row), pl.BlockSpec((bm, nk * HEAD_DIM), row),
                   pl.BlockSpec((bm, 2 * nk * HEAD_DIM), row)],
        out_shape=[jax.ShapeDtypeStruct((t, nq * HEAD_DIM), BF16), jax.ShapeDtypeStruct((t, nk * HEAD_DIM), BF16),
                   jax.ShapeDtypeStruct((t, 2 * nk * HEAD_DIM), BF16)],
        compiler_params=_params("parallel"),
        name="rope_qk",
    )(qkv, cos, sin, q_gain.reshape(1, -1).astype(F32), k_gain.reshape(1, -1).astype(F32))


def rope_tables(seqs):
    half = HEAD_DIM // 2
    inv = ROPE_THETA ** (-jnp.arange(0, half, 2, dtype=F32) / half)
    t = jnp.arange(seqs.s1)
    ang_r = (t // GRID_W).astype(F32)[:, None] * inv
    ang_c = (t % GRID_W).astype(F32)[:, None] * inv
    cr, sr, cc, sc = jnp.cos(ang_r), jnp.sin(ang_r), jnp.cos(ang_c), jnp.sin(ang_c)
    return jnp.concatenate([cr, cr, cc, cc], axis=-1), jnp.concatenate([-sr, sr, -sc, sc], axis=-1)


def _kv_window_spec(seqs, bq, width, col0):
    return pl.BlockSpec((pl.Element(seqs.s1), pl.Element(width)),
                        lambda h, i: (pl.multiple_of(seqs.bounds(i * bq)[0], bq), pl.multiple_of(col0(h), LANES)))


GQA_ROW_CHUNK = 32
DIFF_ROW_CHUNK = 16


def _flash_pipeline(nblk, scores, probs, pv_scale, s_bufs, p_bufs):
    (s_e, s_o), (p_e, p_o) = s_bufs, p_bufs
    scores(0, s_e)
    scores(1, s_o)
    probs(0, s_e, p_e)

    def pair(jj, carry):
        j = 2 * jj + 1
        scores(j + 1, s_e)
        probs(j, s_o, p_o)
        pv_scale(j - 1, p_e, True)
        scores(j + 2, s_o)
        probs(j + 1, s_e, p_e)
        pv_scale(j, p_o, True)
        return carry

    lax.fori_loop(0, (nblk - 2) // 2, pair, 0)
    probs(nblk - 1, s_o, p_o)
    pv_scale(nblk - 2, p_e, True)
    pv_scale(nblk - 1, p_o, False)


def _flash_scratch(m_rows, bkv, acc_width, n_stats):
    return ([pltpu.VMEM((m_rows, bkv), F32)] * 2 + [pltpu.VMEM((m_rows, bkv), BF16)] * 2
            + [pltpu.VMEM((m_rows, acc_width), F32)] + [pltpu.VMEM((m_rows, 1), F32)] * n_stats)


def _gqa_body(q_ref, k_ref, v_ref, o_ref, s_e, s_o, p_e, p_o, acc_ref, m_ref, alpha_ref, *, rep, bq, bkv, seqs):
    i = pl.program_id(1)
    nblk = seqs.bounds(i * bq)[1] // bkv
    q = q_ref[...]
    qs = jnp.concatenate([q[:, r * HEAD_DIM:(r + 1) * HEAD_DIM] for r in range(rep)], axis=0)
    rows = lambda j: pl.ds(pl.multiple_of(j * bkv, bkv), bkv)

    def scores(j, s_ref):
        s_ref[...] = lax.dot_general(qs, k_ref[rows(j), :], NT_DIMS, preferred_element_type=F32)

    def probs(j, s_ref, p_ref):
        for r0 in range(0, rep * bq, GQA_ROW_CHUNK):
            rs = slice(r0, r0 + GQA_ROW_CHUNK)
            s = s_ref[rs, :]
            m = m_ref[rs, :]
            m_new = jnp.maximum(m, jnp.max(s, axis=-1, keepdims=True))
            p_ref[rs, :] = jnp.exp2(s - m_new).astype(BF16)
            alpha_ref[rs, :] = jnp.exp2(m - m_new)
            m_ref[rs, :] = m_new

    def pv_scale(j, p_ref, rescale):
        acc = acc_ref[...] + jnp.dot(p_ref[...], v_ref[rows(j), :], preferred_element_type=F32)
        acc_ref[...] = acc * alpha_ref[...] if rescale else acc

    acc_ref[...] = jnp.zeros_like(acc_ref)
    m_ref[...] = jnp.full_like(m_ref, MASK_VALUE)
    _flash_pipeline(nblk, scores, probs, pv_scale, (s_e, s_o), (p_e, p_o))
    acc = acc_ref[...]
    o = acc[:, :HEAD_DIM] / acc[:, HEAD_DIM:]
    for r in range(rep):
        o_ref[:, r * HEAD_DIM:(r + 1) * HEAD_DIM] = o[r * bq:(r + 1) * bq].astype(o_ref.dtype)


def gqa_attention(q, k, v1, seqs, bq_pref=256, bkv_pref=1024):
    t, wq = q.shape
    nk = k.shape[1] // HEAD_DIM
    rep = wq // HEAD_DIM // nk
    bq = _pick(seqs.s2, bq_pref, 16)
    bkv = _pick(seqs.s2 // 2, bkv_pref)
    return pl.pallas_call(
        functools.partial(_gqa_body, rep=rep, bq=bq, bkv=bkv, seqs=seqs),
        grid=(nk, t // bq),
        in_specs=[pl.BlockSpec((bq, rep * HEAD_DIM), lambda g, i: (i, g)),
                  _kv_window_spec(seqs, bq, HEAD_DIM, lambda g: g * HEAD_DIM),
                  _kv_window_spec(seqs, bq, 2 * HEAD_DIM, lambda g: g * 2 * HEAD_DIM)],
        out_specs=pl.BlockSpec((bq, rep * HEAD_DIM), lambda g, i: (i, g)),
        out_shape=jax.ShapeDtypeStruct((t, wq), BF16),
        scratch_shapes=_flash_scratch(rep * bq, bkv, 2 * HEAD_DIM, 2),
        compiler_params=_params("parallel", "arbitrary"),
        name="gqa_attention",
    )(q, k, v1)


def _diff_body(q_ref, k_ref, v_ref, slope_ref, tab_ref, lq1_ref, lk1_ref, lq2_ref, lk2_ref, g_ref, o_ref,
               s_e, s_o, p_e, p_o, acc_ref, m_ref, alpha_ref, l_ref, *, bq, bkv, seqs, lambda_init):
    i = pl.program_id(1)
    start, slen = seqs.bounds(i * bq)
    nblk = slen // bkv
    qpos0 = i * bq - start
    q = q_ref[...]
    q0, q1 = q[:, :HEAD_DIM], q[:, HEAD_DIM:]
    slope = slope_ref[0][:, :1] * LOG2E
    rows = lambda j: pl.ds(pl.multiple_of(j * bkv, bkv), bkv)

    def scores(j, s_ref):
        k = k_ref[rows(j), :]
        s_ref[:bq] = lax.dot_general(q0, k[:, :HEAD_DIM], NT_DIMS, preferred_element_type=F32)
        s_ref[bq:] = lax.dot_general(q1, k[:, HEAD_DIM:], NT_DIMS, preferred_element_type=F32)

    def probs(j, s_ref, p_ref):
        subs = []
        for c0 in range(0, bkv, bq):
            lead = qpos0 - (j * bkv + c0)
            kind = jnp.where(lead == 0, 2, jnp.where(lead > 0, 0, 1))
            subs.append((c0, kind, slope * jnp.abs(lead).astype(F32)))
        for r0 in range(0, 2 * bq, DIFF_ROW_CHUNK):
            rs = slice(r0, r0 + DIFF_ROW_CHUNK)
            rq = slice(r0 % bq, r0 % bq + DIFF_ROW_CHUNK)
            ss = [s_ref[rs, c0:c0 + bq] - tab_ref[0, kind, rq, :] for c0, kind, _ in subs]
            mx = functools.reduce(jnp.maximum, [jnp.max(s, axis=-1, keepdims=True) - shift
                                                for s, (_, _, shift) in zip(ss, subs)])
            m = m_ref[rs, :]
            m_new = jnp.maximum(m, mx)
            alpha = jnp.exp2(m - m_new)
            lsum = alpha * l_ref[rs, :]
            for s, (c0, _, shift) in zip(ss, subs):
                p = jnp.exp2(s - (m_new + shift))
                p_ref[rs, c0:c0 + bq] = p.astype(BF16)
                lsum = lsum + sum(p[:, c:c + LANES] for c in range(0, bq, LANES))
            l_ref[rs, :] = lsum
            alpha_ref[rs, :] = alpha
            m_ref[rs, :] = m_new

    def pv_scale(j, p_ref, rescale):
        acc = acc_ref[...] + jnp.dot(p_ref[...], v_ref[rows(j), :], preferred_element_type=F32)
        acc_ref[...] = acc * alpha_ref[...] if rescale else acc

    acc_ref[...] = jnp.zeros_like(acc_ref)
    m_ref[...] = jnp.full_like(m_ref, MASK_VALUE)
    l_ref[...] = jnp.zeros_like(l_ref)
    _flash_pipeline(nblk, scores, probs, pv_scale, (s_e, s_o), (p_e, p_o))
    o = acc_ref[...] / jnp.sum(l_ref[...], axis=-1, keepdims=True)
    lam = (jnp.exp(jnp.sum(lq1_ref[...] * lk1_ref[...], axis=-1, keepdims=True))
           - jnp.exp(jnp.sum(lq2_ref[...] * lk2_ref[...], axis=-1, keepdims=True)) + lambda_init)
    d = o[:bq] - lam * o[bq:]
    ms = jnp.mean(d * d, axis=-1, keepdims=True)
    o_ref[...] = (d * lax.rsqrt(ms + RMS_EPS) * g_ref[...] * (1.0 - lambda_init)).astype(o_ref.dtype)


def diff_attention(qkv, slopes, lq1, lk1, lq2, lk2, subln_gain, seqs, lambda_init, bq_pref=512, bkv_pref=1024):
    t, w = qkv.shape
    hw = 2 * HEAD_DIM
    nh = w // (3 * hw)
    bq = _pick(seqs.s2 // 2, bq_pref)
    bkv = _pick(seqs.s2 // 2, bkv_pref)
    assert bkv % bq == 0
    rel = (jnp.arange(bq)[:, None] - jnp.arange(bq)[None, :]).astype(F32) * (LOG2E * slopes.astype(F32))[:, None, None]
    tabs = jnp.stack([rel, -rel, jnp.abs(rel)], axis=1)
    vec = lambda a: a.reshape(1, -1).astype(F32)
    vec_spec = lambda n: pl.BlockSpec((1, n), lambda h, i: (0, 0))
    return pl.pallas_call(
        functools.partial(_diff_body, bq=bq, bkv=bkv, seqs=seqs, lambda_init=lambda_init),
        grid=(nh, t // bq),
        in_specs=[pl.BlockSpec((bq, hw), lambda h, i: (i, h)),
                  _kv_window_spec(seqs, bq, hw, lambda h: (nh + h) * hw),
                  _kv_window_spec(seqs, bq, hw, lambda h: (2 * nh + h) * hw),
                  pl.BlockSpec((1, 1, LANES), lambda h, i: (h, 0, 0)),
                  pl.BlockSpec((1, 3, bq, bq), lambda h, i: (h, 0, 0, 0)),
                  vec_spec(HEAD_DIM), vec_spec(HEAD_DIM), vec_spec(HEAD_DIM), vec_spec(HEAD_DIM), vec_spec(hw)],
        out_specs=pl.BlockSpec((bq, hw), lambda h, i: (i, h)),
        out_shape=jax.ShapeDtypeStruct((t, nh * hw), BF16),
        scratch_shapes=_flash_scratch(2 * bq, bkv, hw, 2) + [pltpu.VMEM((2 * bq, LANES), F32)],
        compiler_params=_params("parallel", "arbitrary"),
        name="diff_attention",
    )(qkv, qkv, qkv, jnp.broadcast_to(slopes.astype(F32)[:, None, None], (nh, 1, LANES)), tabs,
      vec(lq1), vec(lk1), vec(lq2), vec(lk2), vec(subln_gain))


def _dilated_body(q_ref, kp_ref, kc_ref, kn_ref, vp_ref, vc_ref, vn_ref, o_ref, lse_ref, *, nh, dilation, radius, seqs):
    n = pl.program_id(1)
    qb, hb = C_Q_BLOCK, C_HALO_BLOCK
    row0 = n * qb
    seq_start, seq_len = seqs.bounds(row0, dilation)
    kw = qb + 2 * radius
    r_i = lax.broadcasted_iota(jnp.int32, (qb, kw), 0)
    c_i = lax.broadcasted_iota(jnp.int32, (qb, kw), 1)
    jrel = c_i - radius - r_i
    kabs = row0 - radius + c_i
    valid = (jnp.abs(jrel) <= radius) & (kabs >= seq_start) & (kabs < seq_start + seq_len)
    dist = (dilation * jnp.abs(jrel)).astype(F32)
    lane = lax.broadcasted_iota(jnp.int32, (qb, LANES), 1)
    lse_tile = jnp.zeros((qb, LANES), F32)
    sls = [slice(h * HEAD_DIM, (h + 1) * HEAD_DIM) for h in range(nh)]
    window = lambda p_ref, c_ref, n_ref, sl: jnp.concatenate([p_ref[hb - radius:, sl], c_ref[:, sl], n_ref[:radius, sl]],
                                                             axis=0)
    score = lambda h: lax.dot_general(q_ref[:, sls[h]], window(kp_ref, kc_ref, kn_ref, sls[h]), NT_DIMS,
                                      preferred_element_type=F32)
    s_next = score(0)
    for h, sl in enumerate(sls):
        s = s_next
        if h + 1 < nh:
            s_next = score(h + 1)
        slope = LOG2E * 2.0 ** (-8.0 * (h + 1) / nh)
        v = window(vp_ref, vc_ref, vn_ref, sl)
        s = jnp.where(valid, s - slope * dist, MASK_VALUE)
        m = jnp.max(s, axis=-1, keepdims=True)
        p = jnp.exp2(s - m)
        l = jnp.sum(p, axis=-1, keepdims=True)
        o_ref[:, sl] = jnp.dot(p.astype(BF16), v, preferred_element_type=F32) / l
        lse_tile = jnp.where(lane == h, m + jnp.log2(l), lse_tile)
    lse_ref[...] = lse_tile


def residue_major(a, dilation):
    t = a.shape[0]
    return a if dilation == 1 else a.reshape(t // dilation, dilation, -1).swapaxes(0, 1).reshape(t, -1)


def dilated_group(qkv, nh, window, dilation, seqs):
    t, w = qkv.shape
    hw = nh * HEAD_DIM
    radius = window // (2 * dilation)
    qb, hb = C_Q_BLOCK, C_HALO_BLOCK
    assert radius <= hb and qb % hb == 0 and seqs.s2 % (dilation * qb) == 0
    rows = t // dilation
    nblk, nhalo = rows // qb, rows // hb

    def centre(which):
        return pl.BlockSpec((qb, hw), lambda c, n: (c * nblk + n, which))

    def halo(which, side):
        first = lambda n: n * (qb // hb) - 1 if side == 0 else (n + 1) * (qb // hb)
        return pl.BlockSpec((hb, hw), lambda c, n: (c * nhalo + jnp.clip(first(n), 0, nhalo - 1), which))

    o, lse = pl.pallas_call(
        functools.partial(_dilated_body, nh=nh, dilation=dilation, radius=radius, seqs=seqs),
        grid=(dilation, nblk),
        in_specs=[centre(0), halo(1, 0), centre(1), halo(1, 1), halo(2, 0), centre(2), halo(2, 1)],
        out_specs=[pl.BlockSpec((qb, hw), lambda c, n: (n, c)),
                   pl.BlockSpec((qb, LANES), lambda c, n: (n, c))],
        out_shape=[jax.ShapeDtypeStruct((rows, dilation * hw), F32),
                   jax.ShapeDtypeStruct((rows, dilation * LANES), F32)],
        compiler_params=_params("parallel", "parallel"),
        name=f"dilated_d{dilation}",
    )(qkv, qkv, qkv, qkv, qkv, qkv, qkv)
    return o.reshape(t, hw), lse.reshape(t, LANES)


def _merge_body(*refs, ng, nh):
    o_refs, lse_refs, out_ref = refs[:ng], refs[ng:2 * ng], refs[2 * ng]
    lses = [r[...] for r in lse_refs]
    mx = functools.reduce(jnp.maximum, lses)
    es = [jnp.exp2(x - mx) for x in lses]
    tot = functools.reduce(lambda a, b: a + b, es)
    ws = [e / tot for e in es]
    for h in range(nh):
        sl = slice(h * HEAD_DIM, (h + 1) * HEAD_DIM)
        acc = ws[0][:, h:h + 1] * o_refs[0][:, sl]
        for gi in range(1, ng):
            acc = acc + ws[gi][:, h:h + 1] * o_refs[gi][:, sl]
        out_ref[:, sl] = acc.astype(out_ref.dtype)


def merge_groups(outs, lses, nh):
    t, hw = outs[0].shape
    ng = len(outs)
    bm = _pick(t, 256, 8)
    return pl.pallas_call(
        functools.partial(_merge_body, ng=ng, nh=nh),
        grid=(t // bm,),
        in_specs=[pl.BlockSpec((bm, hw), lambda i: (i, 0))] * ng + [pl.BlockSpec((bm, LANES), lambda i: (i, 0))] * ng,
        out_specs=pl.BlockSpec((bm, hw), lambda i: (i, 0)),
        out_shape=jax.ShapeDtypeStruct((t, hw), BF16),
        compiler_params=_params("parallel"),
        name="dilated_merge",
    )(*outs, *lses)


def _na_block_maps(seqs):
    def maps(rb):
        r0 = rb * NA_BLOCK_ROWS
        start, nrows = seqs.bounds(r0, GRID_W)
        rl = r0 - start
        ws = start + jnp.clip(rl - NA_ROWS // 2, 0, nrows - NA_WIN_ROWS)
        variant = jnp.where(rl == 0, 0, jnp.where(rl == nrows - NA_BLOCK_ROWS, 2, 1))
        return ws, variant

    return maps


def _na_body(q_ref, k_ref, v_ref, b_ref, o_ref, *, nh):
    sls = [slice(h * HEAD_DIM, (h + 1) * HEAD_DIM) for h in range(nh)]
    score = lambda h: lax.dot_general(q_ref[:, sls[h]], k_ref[:, sls[h]], NT_DIMS, preferred_element_type=F32) + b_ref[0, h]
    s_next = score(0)
    for h, sl in enumerate(sls):
        s = s_next
        if h + 1 < nh:
            s_next = score(h + 1)
        m = jnp.max(s, axis=-1, keepdims=True)
        p = jnp.exp2(s - m)
        l = jnp.sum(p, axis=-1, keepdims=True)
        o_ref[:, sl] = (jnp.dot(p.astype(BF16), v_ref[:, sl], preferred_element_type=F32) / l).astype(o_ref.dtype)


def na_bias_tables(rpb):
    c = jnp.arange(GRID_W)
    cs = jnp.clip(c - NA_COLS // 2, 0, GRID_W - NA_COLS)
    col_ok = (c[None, :] >= cs[:, None]) & (c[None, :] < cs[:, None] + NA_COLS)
    col_idx = jnp.clip(c[None, :] - c[:, None] + NA_COLS - 1, 0, 2 * NA_COLS - 2)
    rpb_c = jnp.where(col_ok[None, None], rpb.astype(F32)[:, :, col_idx] * LOG2E, MASK_VALUE)
    q = np.arange(NA_BLOCK_ROWS)
    half = NA_ROWS // 2
    variants = [(np.zeros_like(q), q - half),
                (q, np.zeros_like(q)),
                (np.full_like(q, NA_WIN_ROWS - NA_ROWS), q)]
    kr = np.arange(NA_WIN_ROWS)
    tabs = []
    for off, e in variants:
        rr = kr[None, :] - off[:, None]
        valid = (rr >= 0) & (rr < NA_ROWS)
        row_off = np.clip(rr + half - 1 - e[:, None], 0, 2 * NA_ROWS - 2)
        tab = jnp.where(jnp.asarray(valid)[None, :, :, None, None], rpb_c[:, row_off], MASK_VALUE)
        tabs.append(jnp.transpose(tab, (0, 1, 3, 2, 4)).reshape(rpb.shape[0], NA_BLOCK_ROWS * GRID_W,
                                                               NA_WIN_ROWS * GRID_W))
    return jnp.stack(tabs)


def neighbourhood_attention(qkv, bias, nh, seqs):
    t, w = qkv.shape
    hw = nh * HEAD_DIM
    hps = min(NA_HEADS_PER_STEP, nh)
    gw = hps * HEAD_DIM
    assert seqs.s2 % (NA_BLOCK_ROWS * GRID_W) == 0 and seqs.s2 >= NA_WIN_ROWS * GRID_W
    maps = _na_block_maps(seqs)
    bq = NA_BLOCK_ROWS * GRID_W
    kw = NA_WIN_ROWS * GRID_W

    def win_spec(col0):
        return pl.BlockSpec((pl.Element(kw), pl.Element(gw)),
                            lambda g, rb: (pl.multiple_of(maps(rb)[0] * GRID_W, GRID_W), pl.multiple_of(col0 + g * gw, LANES)))

    return pl.pallas_call(
        functools.partial(_na_body, nh=hps),
        grid=(nh // hps, t // bq),
        in_specs=[pl.BlockSpec((bq, gw), lambda g, rb: (rb, g)),
                  win_spec(hw), win_spec(2 * hw),
                  pl.BlockSpec((1, hps, bq, kw), lambda g, rb: (maps(rb)[1], g, 0, 0))],
        out_specs=pl.BlockSpec((bq, gw), lambda g, rb: (rb, g)),
        out_shape=jax.ShapeDtypeStruct((t, hw), BF16),
        compiler_params=_params("parallel", "arbitrary"),
        name="neighbourhood_attention",
    )(qkv, qkv, qkv, bias)


def _fold_qkv(w, gain, widths, scale):
    col_scale = np.concatenate([np.full((wd,), scale if is_q else 1.0, np.float32) for is_q, wd in widths])
    return (w * (gain.astype(F32)[:, None] * col_scale[None, :])).astype(BF16)


def kernel(x_prompt, x_sample, norm_mix, norm_ffn, norm_final, a_w_qkv, a_q_gain, a_k_gain, a_w_o, b_w_qkv,
           b_lambda_q1, b_lambda_k1, b_lambda_q2, b_lambda_k2, b_subln_gain, b_w_o, c_w_qkv, c_w_o, d_w_qkv, d_rpb,
           d_w_o, ffn_w_gate, ffn_w_up, ffn_w_down):
    b1, s1, d = x_prompt.shape
    b2, s2, _ = x_sample.shape
    seqs = Seqs(b1, s1, b2, s2)
    depth = norm_mix.shape[0]

    a_nk = (a_w_qkv.shape[1] - d) // (2 * HEAD_DIM)
    a_nq = d // HEAD_DIM
    b_nh = b_w_o.shape[0] // (2 * HEAD_DIM)
    c_nh = c_w_o.shape[0] // HEAD_DIM
    d_nh = d_w_o.shape[0] // HEAD_DIM
    ng = len(C_CONFIGS)
    chw = c_nh * HEAD_DIM
    qscale = LOG2E * HEAD_DIM ** -0.5
    qkv_cols = {0: [(False, a_w_qkv.shape[1])],
                1: [(True, b_nh * 2 * HEAD_DIM), (False, 2 * b_nh * 2 * HEAD_DIM)],
                2: [(True, chw), (False, 2 * chw)] * ng,
                3: [(True, d_nh * HEAD_DIM), (False, 2 * d_nh * HEAD_DIM)]}
    w_qkv = {0: a_w_qkv, 1: b_w_qkv, 2: c_w_qkv, 3: d_w_qkv}
    w_o = {0: a_w_o, 1: b_w_o, 2: c_w_o, 3: d_w_o}

    cos, sin = rope_tables(seqs)
    slopes_b = 2.0 ** (-8.0 * jnp.arange(1, b_nh + 1, dtype=F32) / b_nh)
    na_bias = na_bias_tables(d_rpb)

    x, xb, rstd = prep_stream(x_sample.reshape(b2 * s2, d), x_prompt.reshape(b1 * s1, d))
    for i in range(depth):
        kind = i % 4
        fold = lambda w, cols: _fold_qkv(w, norm_mix[i], cols, qscale)
        wq = fold(w_qkv[kind], qkv_cols[kind]) if kind != 2 else None
        if kind == 0:
            qkv = matmul_scaled(xb, wq, rstd, F32)
            q, k, v1 = rope_qk(qkv, cos, sin, a_q_gain, a_k_gain, a_nq, a_nk, seqs)
            o = gqa_attention(q, k, v1, seqs)
        elif kind == 1:
            qkv = matmul_scaled(xb, wq, rstd, BF16)
            lambda_init = 0.8 - 0.6 * math.exp(-0.3 * i)
            o = diff_attention(qkv, slopes_b, b_lambda_q1, b_lambda_k1, b_lambda_q2, b_lambda_k2, b_subln_gain, seqs,
                               lambda_init)
        elif kind == 2:
            group_w = lambda g: fold(c_w_qkv[:, g * 3 * chw:(g + 1) * 3 * chw], qkv_cols[kind][:2])
            group_qkv = lambda g, dil: matmul_scaled(residue_major(xb, dil), group_w(g), residue_major(rstd, dil), BF16)
            outs, lses = zip(*[dilated_group(group_qkv(g, dil), c_nh, win, dil, seqs)
                               for g, (win, dil) in enumerate(C_CONFIGS)])
            o = merge_groups(outs, lses, c_nh)
        else:
            qkv = matmul_scaled(xb, wq, rstd, BF16)
            o = neighbourhood_attention(qkv, na_bias, d_nh, seqs)
        x, xb, rstd = matmul_res(o, w_o[kind].astype(BF16), x)
        gain = norm_ffn[i].astype(F32)[:, None]
        a = gateup(xb, (ffn_w_gate[i] * gain).astype(BF16), (ffn_w_up[i] * gain).astype(BF16), rstd)
        x, xb, rstd = matmul_res(a, ffn_w_down[i].astype(BF16), x, bm_pref=512)

    y_sample = rmsnorm(x, norm_final, F32, row0=0, rows=b2 * s2).reshape(b2, s2, d)
    y_prompt = rmsnorm(x, norm_final, F32, row0=b2 * s2, rows=b1 * s1).reshape(b1, s1, d)
    return y_prompt, y_sample
```
